```python
import jax, jax.numpy as jnp
from jax import lax
import numpy as np

D_MODEL = 1024
BATCH = 16
SEQ = 2048
DEPTH = 4

GRID_W = 64
CTX_LEN = 256
N_MIXERS = 3
BRANCH = D_MODEL
N_HEADS = 16
N_KV_HEADS = 4
HEAD_DIM = BRANCH // N_HEADS
GQA_GROUP = N_HEADS // N_KV_HEADS
ROPE_FREQS = HEAD_DIM // 4
ROPE_THETA = 10000.0
Q_BLOCK = 128
FOURIER_GROUPS = 4
FOURIER_GROUP_DIM = BRANCH // FOURIER_GROUPS
CONV_WIDTH = 31
CONV_PAD = CONV_WIDTH // 2
EPS = 1e-6

kernel_name = "hybrid_interleaved_dit_trunk"


def _rmsnorm(t, w):
    tf = t.astype(jnp.float32)
    y = tf * lax.rsqrt(jnp.mean(tf * tf, axis=-1, keepdims=True) + EPS)
    return (y * w.astype(jnp.float32)).astype(t.dtype)


def _layernorm(t, w, b):
    tf = t.astype(jnp.float32)
    mu = jnp.mean(tf, axis=-1, keepdims=True)
    var = jnp.mean(jnp.square(tf - mu), axis=-1, keepdims=True)
    y = (tf - mu) * lax.rsqrt(var + EPS)
    return (y * w.astype(jnp.float32) + b.astype(jnp.float32)).astype(t.dtype)


def _axial_rope_tables(n_tokens, rows, dtype):
    row_idx, col_idx = jnp.meshgrid(jnp.arange(rows), jnp.arange(GRID_W), indexing="ij")
    pos = jnp.stack([row_idx.reshape(-1), col_idx.reshape(-1)], axis=-1).astype(jnp.float32)
    inv_freq = ROPE_THETA ** (-jnp.arange(ROPE_FREQS, dtype=jnp.float32) / ROPE_FREQS)
    ang = pos[:, :, None] * inv_freq
    return jnp.cos(ang).astype(dtype), jnp.sin(ang).astype(dtype)


def _apply_axial_rope(t, cos, sin):
    ts = t.reshape(*t.shape[:-1], 2, 2, ROPE_FREQS)
    t1, t2 = ts[..., 0, :], ts[..., 1, :]
    cb, sb = cos[None, :, None], sin[None, :, None]
    out = jnp.stack([t1 * cb - t2 * sb, t2 * cb + t1 * sb], axis=-2)
    return out.reshape(t.shape)


def _gqa(q, k, v):
    s = jnp.einsum("bqkgd,bskd->bkgqs", q, k) * (HEAD_DIM ** -0.5)
    p = jax.nn.softmax(s.astype(jnp.float32), axis=-1).astype(v.dtype)
    return jnp.einsum("bkgqs,bskd->bqkgd", p, v)


def _attention_mixer(h, hc, cos, sin, w_in, q_norm, k_norm, w_out, need_ctx):
    q_w, kv_w = N_HEADS * HEAD_DIM, N_KV_HEADS * HEAD_DIM

    def project(t):
        p = t @ w_in
        q, k, v, z = jnp.split(p, [q_w, q_w + kv_w, q_w + 2 * kv_w], axis=-1)
        lead = t.shape[:-1]
        q = _rmsnorm(q.reshape(*lead, N_HEADS, HEAD_DIM), q_norm)
        k = _rmsnorm(k.reshape(*lead, N_KV_HEADS, HEAD_DIM), k_norm)
        v = v.reshape(*lead, N_KV_HEADS, HEAD_DIM)
        return q, k, v, z

    q, k, v, z = project(h)
    qc, kc, vc, zc = project(hc)
    q = _apply_axial_rope(q, cos, sin)
    k = _apply_axial_rope(k, cos, sin)
    k_all = jnp.concatenate([kc, k], axis=1)
    v_all = jnp.concatenate([vc, v], axis=1)
    B, S = h.shape[0], h.shape[1]
    nb = S // Q_BLOCK
    qb = q.reshape(B, nb, Q_BLOCK, N_KV_HEADS, GQA_GROUP, HEAD_DIM).transpose(1, 0, 2, 3, 4, 5)
    o = lax.map(lambda qblk: _gqa(qblk, k_all, v_all), qb)
    o = o.transpose(1, 0, 2, 3, 4, 5).reshape(B, S, q_w)
    y = (o * jax.nn.silu(z)) @ w_out
    yc = None
    if need_ctx:
        Lc = hc.shape[1]
        oc = _gqa(qc.reshape(B, Lc, N_KV_HEADS, GQA_GROUP, HEAD_DIM), kc, vc).reshape(B, Lc, q_w)
        yc = (oc * jax.nn.silu(zc)) @ w_out
    return y, yc


def _fourier_mixer(h, hc, w_in, w_out, need_ctx):
    def mix(t):
        u, z = jnp.split(t @ w_in, 2, axis=-1)
        ug = u.reshape(*t.shape[:-1], FOURIER_GROUPS, FOURIER_GROUP_DIM).astype(jnp.float32)
        f = jnp.fft.fftn(ug, axes=(1, 3), norm="ortho").real.astype(t.dtype).reshape(u.shape)
        return (f * jax.nn.silu(z)) @ w_out

    y = mix(h)
    yc = mix(hc) if need_ctx else None
    return y, yc


def _conv_mixer(h, hc, w_in, dw_w, dw_b, ln_w, ln_b, w_out, need_ctx):
    dw = dw_w[:, None, :]

    def mix(t):
        a, g, z = jnp.split(t @ w_in, 3, axis=-1)
        u = a * jax.nn.sigmoid(g)
        u = lax.conv_general_dilated(u, dw, window_strides=(1,), padding=[(CONV_PAD, CONV_PAD)],
                                     dimension_numbers=("NWC", "WIO", "NWC"),
                                     feature_group_count=BRANCH) + dw_b
        u = jax.nn.silu(_layernorm(u, ln_w, ln_b))
        return (u * jax.nn.silu(z)) @ w_out

    y = mix(h)
    yc = mix(hc) if need_ctx else None
    return y, yc


def setup_inputs(seed: int = 0) -> dict:
    key = jax.random.key(seed)
    ks = iter(jax.random.split(key, 64))

    def nrm(shape, scale):
        return scale * jax.random.normal(next(ks), shape, jnp.float32)

    attn_in = N_HEADS * HEAD_DIM + 2 * N_KV_HEADS * HEAD_DIM + BRANCH
    inp = {
        "x": nrm((BATCH, SEQ, D_MODEL), 1.0),
        "c": nrm((BATCH, D_MODEL), 1.0),
        "ctx": nrm((BATCH, CTX_LEN, D_MODEL), 1.0),
        "c_ctx": nrm((D_MODEL,), 1.0),
    }
    for i in range(DEPTH):
        kind = i % N_MIXERS
        inp[f"l{i}_ada_w"] = nrm((D_MODEL, 3 * D_MODEL), 0.5 * D_MODEL ** -0.5)
        inp[f"l{i}_ada_b"] = nrm((3 * D_MODEL,), 0.01)
        inp[f"l{i}_norm_w"] = 1.0 + nrm((D_MODEL,), 0.1)
        if kind == 0:
            inp[f"l{i}_w_in"] = nrm((D_MODEL, attn_in), D_MODEL ** -0.5)
            inp[f"l{i}_q_norm"] = 1.0 + nrm((HEAD_DIM,), 0.1)
            inp[f"l{i}_k_norm"] = 1.0 + nrm((HEAD_DIM,), 0.1)
        elif kind == 1:
            inp[f"l{i}_w_in"] = nrm((D_MODEL, 2 * BRANCH), D_MODEL ** -0.5)
        else:
            inp[f"l{i}_w_in"] = nrm((D_MODEL, 3 * BRANCH), D_MODEL ** -0.5)
            inp[f"l{i}_dw_w"] = nrm((CONV_WIDTH, BRANCH), CONV_WIDTH ** -0.5)
            inp[f"l{i}_dw_b"] = nrm((BRANCH,), 0.01)
            inp[f"l{i}_ln_w"] = 1.0 + nrm((BRANCH,), 0.1)
            inp[f"l{i}_ln_b"] = nrm((BRANCH,), 0.01)
        inp[f"l{i}_w_out"] = nrm((BRANCH, D_MODEL), BRANCH ** -0.5)
    return inp


def reference(x, c, ctx, c_ctx,
              l0_ada_w, l0_ada_b, l0_norm_w, l0_w_in, l0_q_norm, l0_k_norm, l0_w_out,
              l1_ada_w, l1_ada_b, l1_norm_w, l1_w_in, l1_w_out,
              l2_ada_w, l2_ada_b, l2_norm_w, l2_w_in, l2_dw_w, l2_dw_b, l2_ln_w, l2_ln_b, l2_w_out,
              l3_ada_w, l3_ada_b, l3_norm_w, l3_w_in, l3_q_norm, l3_k_norm, l3_w_out):
    S = x.shape[1]
    ROWS = S // GRID_W
    cos, sin = _axial_rope_tables(S, ROWS, x.dtype)
    layers = (
        (l0_ada_w, l0_ada_b, l0_norm_w, (l0_w_in, l0_q_norm, l0_k_norm, l0_w_out)),
        (l1_ada_w, l1_ada_b, l1_norm_w, (l1_w_in, l1_w_out)),
        (l2_ada_w, l2_ada_b, l2_norm_w, (l2_w_in, l2_dw_w, l2_dw_b, l2_ln_w, l2_ln_b, l2_w_out)),
        (l3_ada_w, l3_ada_b, l3_norm_w, (l3_w_in, l3_q_norm, l3_k_norm, l3_w_out)),
    )
    xc = ctx
    for i in range(DEPTH):
        ada_w, ada_b, norm_w, mp = layers[i]
        kind = i % N_MIXERS
        need_ctx = i < DEPTH - 1
        shift, scale, gate = jnp.split(jax.nn.silu(c) @ ada_w + ada_b, 3, axis=-1)
        shift_c, scale_c, gate_c = jnp.split(jax.nn.silu(c_ctx) @ ada_w + ada_b, 3, axis=-1)
        h = _rmsnorm(x, norm_w) * (1 + scale[:, None]) + shift[:, None]
        hc = _rmsnorm(xc, norm_w) * (1 + scale_c) + shift_c
        if kind == 0:
            y, yc = _attention_mixer(h, hc, cos, sin, *mp, need_ctx=need_ctx)
        elif kind == 1:
            y, yc = _fourier_mixer(h, hc, *mp, need_ctx=need_ctx)
        else:
            y, yc = _conv_mixer(h, hc, *mp, need_ctx=need_ctx)
        x = x + gate[:, None] * y
        if need_ctx:
            xc = xc + gate_c * yc
    return x
```

```python
import functools
import math

import numpy as np
import jax
import jax.numpy as jnp
from jax import lax
from jax.experimental import pallas as pl
from jax.experimental.pallas import tpu as pltpu

D_MODEL = 1024
N_HEADS = 16
N_KV_HEADS = 4
HEAD_DIM = 64
GQA_GROUP = N_HEADS // N_KV_HEADS
Q_W = N_HEADS * HEAD_DIM
KV_W = N_KV_HEADS * HEAD_DIM
GRID_W = 64
ROPE_FREQS = HEAD_DIM // 4
ROPE_THETA = 10000.0
FOURIER_GROUP_DIM = 256
CONV_WIDTH = 31
CONV_PAD = CONV_WIDTH // 2
EPS = 1e-6

LANES = 128
SUBLANES = 8
MXU_DIM = 256
VMEM_LIMIT = 48 * 1024 * 1024

F32 = jnp.float32
BF16 = jnp.bfloat16

TOKEN_TILE = 256
Q_TILE = 256
CONV_ROWS = 8


def _params(*sem):
    return pltpu.CompilerParams(dimension_semantics=sem, vmem_limit_bytes=VMEM_LIMIT)


def _dot(a, b):
    return jnp.dot(a, b, preferred_element_type=F32)


def _silu(t):
    return t / (1.0 + jnp.exp(-t))


def _sigmoid(t):
    return 1.0 / (1.0 + jnp.exp(-t))


def _split_bf16(t):
    hi = t.astype(BF16)
    lo = (t - hi.astype(F32)).astype(BF16)
    return hi, lo


def _ada_kernel(cc_ref, w_ref, b_ref, o_ref):
    a_hi, a_lo = _split_bf16(_silu(cc_ref[...]))
    w_hi, w_lo = _split_bf16(w_ref[...])
    o_ref[...] = _dot(a_hi, w_hi) + (_dot(a_hi, w_lo) + _dot(a_lo, w_hi)) + b_ref[...]


def _ada(cc, ada_w, ada_b):
    rows = cc.shape[0]
    n = ada_w.shape[1]
    tn = D_MODEL
    return pl.pallas_call(
        _ada_kernel,
        grid=(n // tn,),
        in_specs=[pl.BlockSpec((rows, D_MODEL), lambda j: (0, 0)),
                  pl.BlockSpec((D_MODEL, tn), lambda j: (0, j)),
                  pl.BlockSpec((1, tn), lambda j: (0, j))],
        out_specs=pl.BlockSpec((rows, tn), lambda j: (0, j)),
        out_shape=jax.ShapeDtypeStruct((rows, n), F32),
        compiler_params=_params("arbitrary"),
        name="ada",
    )(cc, ada_w, ada_b.reshape(1, n))


def _modnorm(x, mod, nw):
    r = lax.rsqrt(jnp.mean(x * x, axis=-1, keepdims=True) + EPS)
    shift = mod[:, :D_MODEL]
    scale = mod[:, D_MODEL:2 * D_MODEL]
    return ((x * r) * (nw * (1.0 + scale)) + shift).astype(BF16)


def _mod_index(is_ctx, n_batch):
    if is_ctx:
        return lambda b, i: (n_batch, 0, 0)
    return lambda b, i: (b, 0, 0)


def _tile(t):
    return min(TOKEN_TILE, t)


def _head_rms(t, e_ref, w):
    outs = []
    for c in range(t.shape[1] // MXU_DIM):
        tc = t[:, c * MXU_DIM:(c + 1) * MXU_DIM]
        hi, lo = _split_bf16(tc * tc)
        ms = _dot(hi, e_ref[...]) + _dot(lo, e_ref[...])
        outs.append(tc * lax.rsqrt(ms + EPS))
    return jnp.concatenate(outs, axis=1) * w


def _rope(t, cos, sin):
    lane = lax.broadcasted_iota(jnp.int32, (t.shape[0], LANES), 1)
    low_half = (lane & ROPE_FREQS) == 0
    outs = []
    for c in range(t.shape[1] // LANES):
        tc = t[:, c * LANES:(c + 1) * LANES]
        partner = jnp.where(low_half,
                            pltpu.roll(tc, LANES - ROPE_FREQS, 1),
                            pltpu.roll(tc, ROPE_FREQS, 1))
        outs.append(tc * cos + partner * sin)
    return jnp.concatenate(outs, axis=1)


def _attn_in_kernel(x_ref, mod_ref, nw_ref, w_ref, e_ref, qn_ref, kn_ref, cos_ref, sin_ref,
                    q_ref, k_ref, v_ref, g_ref):
    h = _modnorm(x_ref[0], mod_ref[0], nw_ref[...])
    cos = cos_ref[...]
    sin = sin_ref[...]
    q = _head_rms(_dot(h, w_ref[:, :Q_W]), e_ref, qn_ref[...])
    q_ref[0] = (_rope(q, cos, sin) * (HEAD_DIM ** -0.5)).astype(BF16)
    k = _head_rms(_dot(h, w_ref[:, Q_W:Q_W + KV_W]), e_ref, kn_ref[...])
    k_ref[0] = _rope(k, cos, sin).astype(BF16)
    v_ref[0] = _dot(h, w_ref[:, Q_W + KV_W:Q_W + 2 * KV_W]).astype(BF16)
    g_ref[0] = _silu(_dot(h, w_ref[:, Q_W + 2 * KV_W:])).astype(BF16)


def _attn_in(x, mod, nw, w_bf, e_mat, qn, kn, cos, sin, is_ctx):
    b, t, _ = x.shape
    tm = _tile(t)
    n_in = w_bf.shape[1]
    tok = lambda bb, i: (bb, i, 0)
    const = lambda bb, i: (0, 0)
    return pl.pallas_call(
        _attn_in_kernel,
        grid=(b, t // tm),
        in_specs=[pl.BlockSpec((1, tm, D_MODEL), tok),
                  pl.BlockSpec((1, 1, 3 * D_MODEL), _mod_index(is_ctx, b)),
                  pl.BlockSpec((1, D_MODEL), const),
                  pl.BlockSpec((D_MODEL, n_in), const),
                  pl.BlockSpec((MXU_DIM, MXU_DIM), const),
                  pl.BlockSpec((1, Q_W), const),
                  pl.BlockSpec((1, KV_W), const),
                  pl.BlockSpec((tm, LANES), lambda bb, i: (i, 0)),
                  pl.BlockSpec((tm, LANES), lambda bb, i: (i, 0))],
        out_specs=[pl.BlockSpec((1, tm, Q_W), tok),
                   pl.BlockSpec((1, tm, KV_W), tok),
                   pl.BlockSpec((1, tm, KV_W), tok),
                   pl.BlockSpec((1, tm, Q_W), tok)],
        out_shape=[jax.ShapeDtypeStruct((b, t, Q_W), BF16),
                   jax.ShapeDtypeStruct((b, t, KV_W), BF16),
                   jax.ShapeDtypeStruct((b, t, KV_W), BF16),
                   jax.ShapeDtypeStruct((b, t, Q_W), BF16)],
        compiler_params=_params("arbitrary", "arbitrary"),
        name="attn_in_ctx" if is_ctx else "attn_in",
    )(x, mod, nw, w_bf, e_mat, qn, kn, cos, sin)


def _attn_kernel(*refs, lens):
    n_src = len(lens)
    q_ref, g_ref = refs[0], refs[1]
    kv_refs = refs[2:2 + 2 * n_src]
    o_ref = refs[2 + 2 * n_src]
    k_lo, k_hi, v_lo, v_hi = refs[3 + 2 * n_src:]
    grp = pl.program_id(1)
    n_keys = sum(lens)
    half = LANES // 2

    @pl.when(pl.program_id(2) == 0)
    def _stage():
        lane = lax.broadcasted_iota(jnp.int32, (n_keys, half), 1)
        one_col = jnp.where(lane == 0, 1.0, 0.0).astype(BF16)
        zeros = jnp.zeros((n_keys, half), BF16)
        k_lo[:, half:] = zeros
        k_hi[:, :half] = zeros
        v_lo[:, half:] = one_col
        v_hi[:, :half] = one_col
        for gg in range(N_KV_HEADS):
            @pl.when(grp == gg)
            def _copy():
                off = 0
                for s in range(n_src):
                    kk = kv_refs[2 * s][0, :, gg * HEAD_DIM:(gg + 1) * HEAD_DIM]
                    vv = kv_refs[2 * s + 1][0, :, gg * HEAD_DIM:(gg + 1) * HEAD_DIM]
                    k_lo[off:off + lens[s], :half] = kk
                    k_hi[off:off + lens[s], half:] = kk
                    v_lo[off:off + lens[s], :half] = vv
                    v_hi[off:off + lens[s], half:] = vv
                    off += lens[s]

    lane = lax.broadcasted_iota(jnp.int32, (q_ref.shape[1], LANES), 1)
    low = lane < half
    outs = []
    for pair in range(GQA_GROUP // 2):
        q2 = q_ref[0, :, pair * LANES:(pair + 1) * LANES]
        res = []
        for k_st, v_st in ((k_lo, v_lo), (k_hi, v_hi)):
            s = lax.dot_general(q2, k_st[...], (((1,), (1,)), ((), ())),
                                preferred_element_type=F32)
            p = jnp.exp(s - jnp.max(s, axis=-1, keepdims=True))
            res.append(_dot(p.astype(BF16), v_st[...]))
        o_even = res[0] / res[0][:, half:half + 1]
        o_odd = res[1] / res[1][:, 0:1]
        outs.append(jnp.where(low, o_even, o_odd))
    o = jnp.concatenate(outs, axis=1)
    o_ref[0] = (o * g_ref[0].astype(F32)).astype(BF16)


def _attention(q, g, kvs, name):
    b, t, _ = q.shape
    tq = min(Q_TILE, t)
    lens = tuple(int(k.shape[1]) for k, _ in kvs)
    n_keys = sum(lens)
    blk = lambda bb, gg, i: (bb, i, gg)
    in_specs = [pl.BlockSpec((1, tq, GQA_GROUP * HEAD_DIM), blk),
                pl.BlockSpec((1, tq, GQA_GROUP * HEAD_DIM), blk)]
    args = [q, g]
    for (k, v), ln in zip(kvs, lens):
        in_specs += [pl.BlockSpec((1, ln, KV_W), lambda bb, gg, i: (bb, 0, 0))] * 2
        args += [k, v]
    return pl.pallas_call(
        functools.partial(_attn_kernel, lens=lens),
        grid=(b, N_KV_HEADS, t // tq),
        in_specs=in_specs,
        out_specs=pl.BlockSpec((1, tq, GQA_GROUP * HEAD_DIM), blk),
        out_shape=jax.ShapeDtypeStruct((b, t, Q_W), BF16),
        scratch_shapes=[pltpu.VMEM((n_keys, LANES), BF16)] * 4,
        compiler_params=_params("arbitrary", "arbitrary", "arbitrary"),
        name=name,
    )(*args)


def _proj_out_kernel(a_ref, x_ref, mod_ref, w_ref, o_ref):
    gate = mod_ref[0][:, 2 * D_MODEL:]
    o_ref[0] = x_ref[0] + gate * _dot(a_ref[0], w_ref[...])


def _proj_out(a, x, mod, w_bf, is_ctx, name):
    b, t, _ = x.shape
    tm = _tile(t)
    tok = lambda bb, i: (bb, i, 0)
    return pl.pallas_call(
        _proj_out_kernel,
        grid=(b, t // tm),
        in_specs=[pl.BlockSpec((1, tm, D_MODEL), tok),
                  pl.BlockSpec((1, tm, D_MODEL), tok),
                  pl.BlockSpec((1, 1, 3 * D_MODEL), _mod_index(is_ctx, b)),
                  pl.BlockSpec((D_MODEL, D_MODEL), lambda bb, i: (0, 0))],
        out_specs=pl.BlockSpec((1, tm, D_MODEL), tok),
        out_shape=jax.ShapeDtypeStruct(x.shape, F32),
        compiler_params=_params("arbitrary", "arbitrary"),
        name=name,
    )(a, x, mod, w_bf)


def _dft_mats(n, scale):
    idx = np.arange(n, dtype=np.int64)
    ang = 2.0 * np.pi * ((idx[:, None] * idx[None, :]) % n).astype(np.float64) / n
    return np.cos(ang) * scale, np.sin(ang) * scale


def _fourier_in_kernel(x_ref, mod_ref, nw_ref, w_ref, cc_ref, sc_ref, uc_ref, g_ref):
    h = _modnorm(x_ref[0], mod_ref[0], nw_ref[...])
    u = _dot(h, w_ref[:, :D_MODEL]).astype(BF16)
    ucs, uss = [], []
    for c in range(D_MODEL // FOURIER_GROUP_DIM):
        ug = u[:, c * FOURIER_GROUP_DIM:(c + 1) * FOURIER_GROUP_DIM]
        ucs.append(_dot(ug, cc_ref[...]))
        uss.append(_dot(ug, sc_ref[...]))
    uc_ref[0, 0] = jnp.concatenate(ucs, axis=1).astype(BF16)
    uc_ref[0, 1] = jnp.concatenate(uss, axis=1).astype(BF16)
    g_ref[0] = _silu(_dot(h, w_ref[:, D_MODEL:])).astype(BF16)


def _fourier_in(x, mod, nw, w_bf, cmat, smat, is_ctx):
    b, t, _ = x.shape
    tm = _tile(t)
    tok = lambda bb, i: (bb, i, 0)
    const = lambda bb, i: (0, 0)
    return pl.pallas_call(
        _fourier_in_kernel,
        grid=(b, t // tm),
        in_specs=[pl.BlockSpec((1, tm, D_MODEL), tok),
                  pl.BlockSpec((1, 1, 3 * D_MODEL), _mod_index(is_ctx, b)),
                  pl.BlockSpec((1, D_MODEL), const),
                  pl.BlockSpec((D_MODEL, 2 * D_MODEL), const),
                  pl.BlockSpec((FOURIER_GROUP_DIM, FOURIER_GROUP_DIM), const),
                  pl.BlockSpec((FOURIER_GROUP_DIM, FOURIER_GROUP_DIM), const)],
        out_specs=[pl.BlockSpec((1, 2, tm, D_MODEL), lambda bb, i: (bb, 0, i, 0)),
                   pl.BlockSpec((1, tm, D_MODEL), tok)],
        out_shape=[jax.ShapeDtypeStruct((b, 2, t, D_MODEL), BF16),
                   jax.ShapeDtypeStruct((b, t, D_MODEL), BF16)],
        compiler_params=_params("arbitrary", "arbitrary"),
        name="fourier_in_ctx" if is_ctx else "fourier_in",
    )(x, mod, nw, w_bf, cmat, smat)


def _fourier_out_kernel(dft_ref, uc_ref, g_ref, x_ref, mod_ref, w_ref, o_ref):
    f = _dot(dft_ref[...], uc_ref[0])
    a = (f * g_ref[0].astype(F32)).astype(BF16)
    gate = mod_ref[0][:, 2 * D_MODEL:]
    o_ref[0] = x_ref[0] + gate * _dot(a, w_ref[...])


def _fourier_out(dft, ucs, g, x, mod, w_bf, is_ctx):
    b, t, _ = x.shape
    tm = _tile(t)
    tok = lambda bb, i: (bb, i, 0)
    return pl.pallas_call(
        _fourier_out_kernel,
        grid=(b, t // tm),
        in_specs=[pl.BlockSpec((tm, 2 * t), lambda bb, i: (i, 0)),
                  pl.BlockSpec((1, 2 * t, D_MODEL), lambda bb, i: (bb, 0, 0)),
                  pl.BlockSpec((1, tm, D_MODEL), tok),
                  pl.BlockSpec((1, tm, D_MODEL), tok),
                  pl.BlockSpec((1, 1, 3 * D_MODEL), _mod_index(is_ctx, b)),
                  pl.BlockSpec((D_MODEL, D_MODEL), lambda bb, i: (0, 0))],
        out_specs=pl.BlockSpec((1, tm, D_MODEL), tok),
        out_shape=jax.ShapeDtypeStruct(x.shape, F32),
        compiler_params=_params("arbitrary", "arbitrary"),
        name="fourier_out_ctx" if is_ctx else "fourier_out",
    )(dft, ucs.reshape(b, 2 * t, D_MODEL), g, x, mod, w_bf)


def _conv_in_kernel(x_ref, mod_ref, nw_ref, w_ref, u_ref, g_ref):
    h = _modnorm(x_ref[0], mod_ref[0], nw_ref[...])
    a = _dot(h, w_ref[:, :D_MODEL])
    gl = _dot(h, w_ref[:, D_MODEL:2 * D_MODEL])
    u_ref[0] = a * _sigmoid(gl)
    g_ref[0] = _silu(_dot(h, w_ref[:, 2 * D_MODEL:])).astype(BF16)


def _conv_in(x, mod, nw, w_bf, is_ctx):
    b, t, _ = x.shape
    tm = _tile(t)
    tok = lambda bb, i: (bb, i, 0)
    const = lambda bb, i: (0, 0)
    return pl.pallas_call(
        _conv_in_kernel,
        grid=(b, t // tm),
        in_specs=[pl.BlockSpec((1, tm, D_MODEL), tok),
                  pl.BlockSpec((1, 1, 3 * D_MODEL), _mod_index(is_ctx, b)),
                  pl.BlockSpec((1, D_MODEL), const),
                  pl.BlockSpec((D_MODEL, 3 * D_MODEL), const)],
        out_specs=[pl.BlockSpec((1, tm, D_MODEL), tok),
                   pl.BlockSpec((1, tm, D_MODEL), tok)],
        out_shape=[jax.ShapeDtypeStruct((b, t, D_MODEL), F32),
                   jax.ShapeDtypeStruct((b, t, D_MODEL), BF16)],
        compiler_params=_params("arbitrary", "arbitrary"),
        name="conv_in_ctx" if is_ctx else "conv_in",
    )(x, mod, nw, w_bf)


def _conv_out_kernel(u_ref, g_ref, x_ref, mod_ref, dw_ref, dwb_ref, lnw_ref, lnb_ref, w_ref,
                     o_ref, upad, conv):
    t = u_ref.shape[1]
    tm = o_ref.shape[1]
    halo = 2 * SUBLANES
    i = pl.program_id(1)

    @pl.when(i == 0)
    def _stage():
        upad[0:halo, :] = jnp.zeros((halo, D_MODEL), F32)
        upad[halo:halo + t, :] = u_ref[0]
        upad[halo + t:, :] = jnp.zeros((halo, D_MODEL), F32)

    def sub_block(j, carry):
        base = pl.multiple_of(i * tm + j * CONV_ROWS, CONV_ROWS)
        win = upad[pl.ds(base, CONV_ROWS + 2 * halo), :]
        acc = jnp.zeros((CONV_ROWS, D_MODEL), F32)
        for k in range(CONV_WIDTH):
            lo = halo - CONV_PAD + k
            acc = acc + win[lo:lo + CONV_ROWS, :] * dw_ref[k:k + 1, :]
        conv[pl.ds(pl.multiple_of(j * CONV_ROWS, CONV_ROWS), CONV_ROWS), :] = acc
        return carry

    lax.fori_loop(0, tm // CONV_ROWS, sub_block, 0)

    c = conv[...] + dwb_ref[...]
    mu = jnp.mean(c, axis=-1, keepdims=True)
    cen = c - mu
    var = jnp.mean(cen * cen, axis=-1, keepdims=True)
    ln = cen * lax.rsqrt(var + EPS) * lnw_ref[...] + lnb_ref[...]
    a = (_silu(ln) * g_ref[0].astype(F32)).astype(BF16)
    gate = mod_ref[0][:, 2 * D_MODEL:]
    o_ref[0] = x_ref[0] + gate * _dot(a, w_ref[...])


def _conv_out(u, g, x, mod, dw_w, dw_b, ln_w, ln_b, w_bf, is_ctx):
    b, t, _ = x.shape
    tm = _tile(t)
    tok = lambda bb, i: (bb, i, 0)
    const = lambda bb, i: (0, 0)
    return pl.pallas_call(
        _conv_out_kernel,
        grid=(b, t // tm),
        in_specs=[pl.BlockSpec((1, t, D_MODEL), lambda bb, i: (bb, 0, 0)),
                  pl.BlockSpec((1, tm, D_MODEL), tok),
                  pl.BlockSpec((1, tm, D_MODEL), tok),
                  pl.BlockSpec((1, 1, 3 * D_MODEL), _mod_index(is_ctx, b)),
                  pl.BlockSpec((CONV_WIDTH, D_MODEL), const),
                  pl.BlockSpec((1, D_MODEL), const),
                  pl.BlockSpec((1, D_MODEL), const),
                  pl.BlockSpec((1, D_MODEL), const),
                  pl.BlockSpec((D_MODEL, D_MODEL), const)],
        out_specs=pl.BlockSpec((1, tm, D_MODEL), tok),
        out_shape=jax.ShapeDtypeStruct(x.shape, F32),
        scratch_shapes=[pltpu.VMEM((t + 4 * SUBLANES, D_MODEL), F32),
                        pltpu.VMEM((tm, D_MODEL), F32)],
        compiler_params=_params("arbitrary", "arbitrary"),
        name="conv_out_ctx" if is_ctx else "conv_out",
    )(u, g, x, mod, dw_w, dw_b.reshape(1, -1), ln_w.reshape(1, -1), ln_b.reshape(1, -1), w_bf)


def _rope_tables(n_tokens):
    pos = jnp.arange(n_tokens)
    pos2 = jnp.stack([pos // GRID_W, pos % GRID_W], axis=-1).astype(F32)
    inv_freq = ROPE_THETA ** (-jnp.arange(ROPE_FREQS, dtype=F32) / ROPE_FREQS)
    ang = pos2[:, :, None] * inv_freq
    cos, sin = jnp.cos(ang), jnp.sin(ang)
    cos64 = jnp.concatenate([cos[:, 0], cos[:, 0], cos[:, 1], cos[:, 1]], axis=-1)
    sin64 = jnp.concatenate([-sin[:, 0], sin[:, 0], -sin[:, 1], sin[:, 1]], axis=-1)
    return jnp.tile(cos64, (1, 2)), jnp.tile(sin64, (1, 2))


def _head_mean_matrix():
    idx = np.arange(MXU_DIM) // HEAD_DIM
    return jnp.asarray((idx[:, None] == idx[None, :]).astype(np.float32) / HEAD_DIM, dtype=BF16)


def kernel(x, c, ctx, c_ctx, l0_ada_w, l0_ada_b, l0_norm_w, l0_w_in, l0_q_norm, l0_k_norm, l0_w_out, l1_ada_w, l1_ada_b, l1_norm_w, l1_w_in, l1_w_out, l2_ada_w, l2_ada_b, l2_norm_w, l2_w_in, l2_dw_w, l2_dw_b, l2_ln_w, l2_ln_b, l2_w_out, l3_ada_w, l3_ada_b, l3_norm_w, l3_w_in, l3_q_norm, l3_k_norm, l3_w_out):
    n_batch, seq, _ = x.shape
    n_ctx = ctx.shape[1]

    rows = -(-(n_batch + 1) // SUBLANES) * SUBLANES
    cc = jnp.concatenate([c, c_ctx[None, :], jnp.zeros((rows - n_batch - 1, D_MODEL), F32)], axis=0)

    def mods(ada_w, ada_b):
        return _ada(cc, ada_w, ada_b).reshape(rows, 1, 3 * D_MODEL)

    cos, sin = _rope_tables(seq)
    cos_ctx = jnp.ones((n_ctx, LANES), F32)
    sin_ctx = jnp.zeros((n_ctx, LANES), F32)
    e_mat = _head_mean_matrix()

    def attention_layer(xl, xc, ada_w, ada_b, nw, w_in, qn, kn, w_out, need_ctx):
        mod = mods(ada_w, ada_b)
        nw2 = nw.reshape(1, -1)
        w_bf = w_in.astype(BF16)
        wo_bf = w_out.astype(BF16)
        qn_t = jnp.tile(qn, N_HEADS).reshape(1, -1)
        kn_t = jnp.tile(kn, N_KV_HEADS).reshape(1, -1)
        q, k, v, g = _attn_in(xl, mod, nw2, w_bf, e_mat, qn_t, kn_t, cos, sin, False)
        qc, kc, vc, gc = _attn_in(xc, mod, nw2, w_bf, e_mat, qn_t, kn_t, cos_ctx, sin_ctx, True)
        a = _attention(q, g, [(kc, vc), (k, v)], "attention")
        xl_new = _proj_out(a, xl, mod, wo_bf, False, "attn_out")
        if need_ctx:
            ac = _attention(qc, gc, [(kc, vc)], "attention_ctx")
            xc = _proj_out(ac, xc, mod, wo_bf, True, "attn_out_ctx")
        return xl_new, xc

    def fourier_layer(xl, xc, ada_w, ada_b, nw, w_in, w_out):
        mod = mods(ada_w, ada_b)
        nw2 = nw.reshape(1, -1)
        w_bf = w_in.astype(BF16)
        wo_bf = w_out.astype(BF16)
        cch, sch = _dft_mats(FOURIER_GROUP_DIM, FOURIER_GROUP_DIM ** -0.5)
        cch, sch = jnp.asarray(cch, dtype=BF16), jnp.asarray(sch, dtype=BF16)
        outs = []
        for t_arr, is_ctx in ((xl, False), (xc, True)):
            t = t_arr.shape[1]
            cp, sp = _dft_mats(t, t ** -0.5)
            dft = jnp.asarray(np.concatenate([cp, -sp], axis=1), dtype=BF16)
            ucs, g = _fourier_in(t_arr, mod, nw2, w_bf, cch, sch, is_ctx)
            outs.append(_fourier_out(dft, ucs, g, t_arr, mod, wo_bf, is_ctx))
        return outs[0], outs[1]

    def conv_layer(xl, xc, ada_w, ada_b, nw, w_in, dw_w, dw_b, ln_w, ln_b, w_out):
        mod = mods(ada_w, ada_b)
        nw2 = nw.reshape(1, -1)
        w_bf = w_in.astype(BF16)
        wo_bf = w_out.astype(BF16)
        outs = []
        for t_arr, is_ctx in ((xl, False), (xc, True)):
            u, g = _conv_in(t_arr, mod, nw2, w_bf, is_ctx)
            outs.append(_conv_out(u, g, t_arr, mod, dw_w, dw_b, ln_w, ln_b, wo_bf, is_ctx))
        return outs[0], outs[1]

    xl, xc = x, ctx
    xl, xc = attention_layer(xl, xc, l0_ada_w, l0_ada_b, l0_norm_w, l0_w_in, l0_q_norm, l0_k_norm,
                             l0_w_out, True)
    xl, xc = fourier_layer(xl, xc, l1_ada_w, l1_ada_b, l1_norm_w, l1_w_in, l1_w_out)
    xl, xc = conv_layer(xl, xc, l2_ada_w, l2_ada_b, l2_norm_w, l2_w_in, l2_dw_w, l2_dw_b,
                        l2_ln_w, l2_ln_b, l2_w_out)
    xl, xc = attention_layer(xl, xc, l3_ada_w, l3_ada_b, l3_norm_w, l3_w_in, l3_q_norm, l3_k_norm,
                             l3_w_out, False)
    return xl
```

```python
import functools
import math

import numpy as np
import jax
import jax.numpy as jnp
from jax import lax
from jax.experimental import pallas as pl
from jax.experimental.pallas import tpu as pltpu

D_MODEL = 1024
N_HEADS = 16
N_KV_HEADS = 4
HEAD_DIM = 64
GQA_GROUP = N_HEADS // N_KV_HEADS
Q_W = N_HEADS * HEAD_DIM
KV_W = N_KV_HEADS * HEAD_DIM
GRID_W = 64
ROPE_FREQS = HEAD_DIM // 4
ROPE_THETA = 10000.0
FOURIER_GROUP_DIM = 256
CONV_WIDTH = 31
CONV_PAD = CONV_WIDTH // 2
EPS = 1e-6

LANES = 128
SUBLANES = 8
MXU_DIM = 256
VMEM_LIMIT = 48 * 1024 * 1024

F32 = jnp.float32
BF16 = jnp.bfloat16

TOKEN_TILE = 256
Q_TILE = 256
CONV_ROWS = 8


def _params(*sem):
    return pltpu.CompilerParams(dimension_semantics=sem, vmem_limit_bytes=VMEM_LIMIT)


def _dot(a, b):
    return jnp.dot(a, b, preferred_element_type=F32)


def _silu(t):
    return t / (1.0 + jnp.exp(-t))


def _sigmoid(t):
    return 1.0 / (1.0 + jnp.exp(-t))


def _split_bf16(t):
    hi = t.astype(BF16)
    lo = (t - hi.astype(F32)).astype(BF16)
    return hi, lo


def _ada_kernel(cc_ref, w_ref, b_ref, o_ref):
    a_hi, a_lo = _split_bf16(_silu(cc_ref[...]))
    w_hi, w_lo = _split_bf16(w_ref[...])
    o_ref[...] = _dot(a_hi, w_hi) + (_dot(a_hi, w_lo) + _dot(a_lo, w_hi)) + b_ref[...]


def _ada(cc, ada_w, ada_b):
    rows = cc.shape[0]
    n = ada_w.shape[1]
    tn = D_MODEL
    return pl.pallas_call(
        _ada_kernel,
        grid=(n // tn,),
        in_specs=[pl.BlockSpec((rows, D_MODEL), lambda j: (0, 0)),
                  pl.BlockSpec((D_MODEL, tn), lambda j: (0, j)),
                  pl.BlockSpec((1, tn), lambda j: (0, j))],
        out_specs=pl.BlockSpec((rows, tn), lambda j: (0, j)),
        out_shape=jax.ShapeDtypeStruct((rows, n), F32),
        compiler_params=_params("arbitrary"),
        name="ada",
    )(cc, ada_w, ada_b.reshape(1, n))


def _modnorm(x, mod, nw):
    r = lax.rsqrt(jnp.mean(x * x, axis=-1, keepdims=True) + EPS)
    shift = mod[:, :D_MODEL]
    scale = mod[:, D_MODEL:2 * D_MODEL]
    return ((x * r) * (nw * (1.0 + scale)) + shift).astype(BF16)


def _mod_index(is_ctx, n_batch):
    if is_ctx:
        return lambda b, i: (n_batch, 0, 0)
    return lambda b, i: (b, 0, 0)


def _tile(t):
    return min(TOKEN_TILE, t)


def _head_rms(t, e_ref, w):
    outs = []
    for c in range(t.shape[1] // MXU_DIM):
        tc = t[:, c * MXU_DIM:(c + 1) * MXU_DIM]
        hi, lo = _split_bf16(tc * tc)
        ms = _dot(hi, e_ref[...]) + _dot(lo, e_ref[...])
        outs.append(tc * lax.rsqrt(ms + EPS))
    return jnp.concatenate(outs, axis=1) * w


def _rope(t, cos, sin):
    lane = lax.broadcasted_iota(jnp.int32, (t.shape[0], LANES), 1)
    low_half = (lane & ROPE_FREQS) == 0
    outs = []
    for c in range(t.shape[1] // LANES):
        tc = t[:, c * LANES:(c + 1) * LANES]
        partner = jnp.where(low_half,
                            pltpu.roll(tc, LANES - ROPE_FREQS, 1),
                            pltpu.roll(tc, ROPE_FREQS, 1))
        outs.append(tc * cos + partner * sin)
    return jnp.concatenate(outs, axis=1)


def _attn_in_kernel(x_ref, mod_ref, nw_ref, w_ref, e_ref, qn_ref, kn_ref, cos_ref, sin_ref,
                    q_ref, k_ref, v_ref, g_ref):
    h = _modnorm(x_ref[0], mod_ref[0], nw_ref[...])
    cos = cos_ref[...]
    sin = sin_ref[...]
    q = _head_rms(_dot(h, w_ref[:, :Q_W]), e_ref, qn_ref[...])
    q_ref[0] = (_rope(q, cos, sin) * (HEAD_DIM ** -0.5)).astype(BF16)
    k = _head_rms(_dot(h, w_ref[:, Q_W:Q_W + KV_W]), e_ref, kn_ref[...])
    k_ref[0] = _rope(k, cos, sin).astype(BF16)
    v_ref[0] = _dot(h, w_ref[:, Q_W + KV_W:Q_W + 2 * KV_W]).astype(BF16)
    g_ref[0] = _silu(_dot(h, w_ref[:, Q_W + 2 * KV_W:])).astype(BF16)


def _attn_in(x, mod, nw, w_bf, e_mat, qn, kn, cos, sin, is_ctx):
    b, t, _ = x.shape
    tm = _tile(t)
    n_in = w_bf.shape[1]
    tok = lambda bb, i: (bb, i, 0)
    const = lambda bb, i: (0, 0)
    return pl.pallas_call(
        _attn_in_kernel,
        grid=(b, t // tm),
        in_specs=[pl.BlockSpec((1, tm, D_MODEL), tok),
                  pl.BlockSpec((1, 1, 3 * D_MODEL), _mod_index(is_ctx, b)),
                  pl.BlockSpec((1, D_MODEL), const),
                  pl.BlockSpec((D_MODEL, n_in), const),
                  pl.BlockSpec((MXU_DIM, MXU_DIM), const),
                  pl.BlockSpec((1, Q_W), const),
                  pl.BlockSpec((1, KV_W), const),
                  pl.BlockSpec((tm, LANES), lambda bb, i: (i, 0)),
                  pl.BlockSpec((tm, LANES), lambda bb, i: (i, 0))],
        out_specs=[pl.BlockSpec((1, tm, Q_W), tok),
                   pl.BlockSpec((1, tm, KV_W), tok),
                   pl.BlockSpec((1, tm, KV_W), tok),
                   pl.BlockSpec((1, tm, Q_W), tok)],
        out_shape=[jax.ShapeDtypeStruct((b, t, Q_W), BF16),
                   jax.ShapeDtypeStruct((b, t, KV_W), BF16),
                   jax.ShapeDtypeStruct((b, t, KV_W), BF16),
                   jax.ShapeDtypeStruct((b, t, Q_W), BF16)],
        compiler_params=_params("arbitrary", "arbitrary"),
        name="attn_in_ctx" if is_ctx else "attn_in",
    )(x, mod, nw, w_bf, e_mat, qn, kn, cos, sin)


V_ROWS = HEAD_DIM + 16
ATTN_VMEM_LIMIT = 56 * 1024 * 1024


def _attn_kernel(*refs, lens, n_qtiles, n_units):
    n_src = len(lens)
    q_ref, g_ref = refs[0], refs[1]
    k_refs = refs[2:2 + n_src]
    v_refs = refs[2 + n_src:2 + 2 * n_src]
    o_ref = refs[2 + 2 * n_src]
    k_lo, k_hi, v_t, s_buf = refs[3 + 2 * n_src:]
    n = pl.program_id(0)
    n_keys = sum(lens)
    half = LANES // 2
    units_per_batch = N_KV_HEADS * n_qtiles

    scored = jnp.minimum(n, n_units - 1)
    emitted = jnp.maximum(n - 1, 0)

    @pl.when(n == 0)
    def _fill():
        s_buf[1] = jnp.zeros(s_buf.shape[1:], F32)

    @pl.when(scored % units_per_batch == 0)
    def _stage_k():
        zeros = jnp.zeros((n_keys, half), BF16)
        for gg in range(N_KV_HEADS):
            k_lo[gg, :, half:] = zeros
            k_hi[gg, :, :half] = zeros
            off = 0
            for s in range(n_src):
                kk = k_refs[s][0, :, gg * HEAD_DIM:(gg + 1) * HEAD_DIM]
                k_lo[gg, off:off + lens[s], :half] = kk
                k_hi[gg, off:off + lens[s], half:] = kk
                off += lens[s]

    @pl.when(emitted % units_per_batch == 0)
    def _stage_v():
        row = lax.broadcasted_iota(jnp.int32, (V_ROWS - HEAD_DIM, n_keys), 0)
        ones_row = jnp.where(row == 0, 1.0, 0.0).astype(BF16)
        for gg in range(N_KV_HEADS):
            v_t[gg, HEAD_DIM:, :] = ones_row
        off = 0
        for s in range(n_src):
            vt = v_refs[s][0].astype(F32).T.astype(BF16)
            for gg in range(N_KV_HEADS):
                v_t[gg, :HEAD_DIM, off:off + lens[s]] = vt[gg * HEAD_DIM:(gg + 1) * HEAD_DIM, :]
            off += lens[s]

    def col_max(t):
        while t.shape[0] % (2 * SUBLANES) == 0:
            m = t.shape[0] // 2
            t = jnp.maximum(t[:m, :], t[m:, :])
        return jnp.max(t, axis=0, keepdims=True)

    def step(write_slot, read_slot):
        grp = (scored // n_qtiles) % N_KV_HEADS
        vt_g = v_t[(emitted // n_qtiles) % N_KV_HEADS]
        o_rows = []
        for pair in range(GQA_GROUP // 2):
            q2 = q_ref[0, :, pair * LANES:(pair + 1) * LANES]
            for h, k_st in ((2 * pair, k_lo), (2 * pair + 1, k_hi)):
                s_buf[write_slot, h] = lax.dot_general(k_st[grp], q2, (((1,), (1,)), ((), ())),
                                                       preferred_element_type=F32)
            for h in (2 * pair, 2 * pair + 1):
                s_t = s_buf[read_slot, h]
                p_t = jnp.exp(s_t - col_max(s_t)).astype(BF16)
                o_t = _dot(vt_g, p_t)
                o_rows.append(o_t[:HEAD_DIM, :] / o_t[HEAD_DIM:HEAD_DIM + 1, :])
        o = jnp.concatenate(o_rows, axis=0).T
        o_ref[0] = (o * g_ref[0].astype(F32)).astype(BF16)

    pl.when(n % 2 == 0)(lambda: step(0, 1))
    pl.when(n % 2 == 1)(lambda: step(1, 0))


def _attention(q, g, kvs, name):
    b, t, _ = q.shape
    tq = min(Q_TILE, t)
    n_qtiles = t // tq
    n_units = b * N_KV_HEADS * n_qtiles
    lens = tuple(int(k.shape[1]) for k, _ in kvs)
    n_keys = sum(lens)

    def unit_block(u):
        return (u // (N_KV_HEADS * n_qtiles), u % n_qtiles, (u // n_qtiles) % N_KV_HEADS)

    def unit_batch(u):
        return (u // (N_KV_HEADS * n_qtiles), 0, 0)

    scored = lambda n: jnp.minimum(n, n_units - 1)
    emitted = lambda n: jnp.maximum(n - 1, 0)
    width = GQA_GROUP * HEAD_DIM
    in_specs = [pl.BlockSpec((1, tq, width), lambda n: unit_block(scored(n))),
                pl.BlockSpec((1, tq, width), lambda n: unit_block(emitted(n)))]
    in_specs += [pl.BlockSpec((1, ln, KV_W), lambda n: unit_batch(scored(n))) for ln in lens]
    in_specs += [pl.BlockSpec((1, ln, KV_W), lambda n: unit_batch(emitted(n))) for ln in lens]
    args = [q, g] + [k for k, _ in kvs] + [v for _, v in kvs]
    return pl.pallas_call(
        functools.partial(_attn_kernel, lens=lens, n_qtiles=n_qtiles, n_units=n_units),
        grid=(n_units + 1,),
        in_specs=in_specs,
        out_specs=pl.BlockSpec((1, tq, width), lambda n: unit_block(emitted(n))),
        out_shape=jax.ShapeDtypeStruct((b, t, Q_W), BF16),
        scratch_shapes=[pltpu.VMEM((N_KV_HEADS, n_keys, LANES), BF16),
                        pltpu.VMEM((N_KV_HEADS, n_keys, LANES), BF16),
                        pltpu.VMEM((N_KV_HEADS, V_ROWS, n_keys), BF16),
                        pltpu.VMEM((2, GQA_GROUP, n_keys, tq), F32)],
        compiler_params=pltpu.CompilerParams(dimension_semantics=("arbitrary",),
                                             vmem_limit_bytes=ATTN_VMEM_LIMIT),
        name=name,
    )(*args)


def _proj_out_kernel(a_ref, x_ref, mod_ref, w_ref, o_ref):
    gate = mod_ref[0][:, 2 * D_MODEL:]
    o_ref[0] = x_ref[0] + gate * _dot(a_ref[0], w_ref[...])


def _proj_out(a, x, mod, w_bf, is_ctx, name):
    b, t, _ = x.shape
    tm = _tile(t)
    tok = lambda bb, i: (bb, i, 0)
    return pl.pallas_call(
        _proj_out_kernel,
        grid=(b, t // tm),
        in_specs=[pl.BlockSpec((1, tm, D_MODEL), tok),
                  pl.BlockSpec((1, tm, D_MODEL), tok),
                  pl.BlockSpec((1, 1, 3 * D_MODEL), _mod_index(is_ctx, b)),
                  pl.BlockSpec((D_MODEL, D_MODEL), lambda bb, i: (0, 0))],
        out_specs=pl.BlockSpec((1, tm, D_MODEL), tok),
        out_shape=jax.ShapeDtypeStruct(x.shape, F32),
        compiler_params=_params("arbitrary", "arbitrary"),
        name=name,
    )(a, x, mod, w_bf)


def _dft_mats(n, scale):
    idx = np.arange(n, dtype=np.int64)
    ang = 2.0 * np.pi * ((idx[:, None] * idx[None, :]) % n).astype(np.float64) / n
    return np.cos(ang) * scale, np.sin(ang) * scale


def _fourier_in_kernel(x_ref, mod_ref, nw_ref, w_ref, cc_ref, sc_ref, uc_ref, g_ref):
    h = _modnorm(x_ref[0], mod_ref[0], nw_ref[...])
    u = _dot(h, w_ref[:, :D_MODEL]).astype(BF16)
    ucs, uss = [], []
    for c in range(D_MODEL // FOURIER_GROUP_DIM):
        ug = u[:, c * FOURIER_GROUP_DIM:(c + 1) * FOURIER_GROUP_DIM]
        ucs.append(_dot(ug, cc_ref[...]))
        uss.append(_dot(ug, sc_ref[...]))
    uc_ref[0, 0] = jnp.concatenate(ucs, axis=1).astype(BF16)
    uc_ref[0, 1] = jnp.concatenate(uss, axis=1).astype(BF16)
    g_ref[0] = _silu(_dot(h, w_ref[:, D_MODEL:])).astype(BF16)


def _fourier_in(x, mod, nw, w_bf, cmat, smat, is_ctx):
    b, t, _ = x.shape
    tm = _tile(t)
    tok = lambda bb, i: (bb, i, 0)
    const = lambda bb, i: (0, 0)
    return pl.pallas_call(
        _fourier_in_kernel,
        grid=(b, t // tm),
        in_specs=[pl.BlockSpec((1, tm, D_MODEL), tok),
                  pl.BlockSpec((1, 1, 3 * D_MODEL), _mod_index(is_ctx, b)),
                  pl.BlockSpec((1, D_MODEL), const),
                  pl.BlockSpec((D_MODEL, 2 * D_MODEL), const),
                  pl.BlockSpec((FOURIER_GROUP_DIM, FOURIER_GROUP_DIM), const),
                  pl.BlockSpec((FOURIER_GROUP_DIM, FOURIER_GROUP_DIM), const)],
        out_specs=[pl.BlockSpec((1, 2, tm, D_MODEL), lambda bb, i: (bb, 0, i, 0)),
                   pl.BlockSpec((1, tm, D_MODEL), tok)],
        out_shape=[jax.ShapeDtypeStruct((b, 2, t, D_MODEL), BF16),
                   jax.ShapeDtypeStruct((b, t, D_MODEL), BF16)],
        compiler_params=_params("arbitrary", "arbitrary"),
        name="fourier_in_ctx" if is_ctx else "fourier_in",
    )(x, mod, nw, w_bf, cmat, smat)


def _fourier_out_kernel(dft_ref, uc_ref, g_ref, x_ref, mod_ref, w_ref, o_ref):
    f = _dot(dft_ref[...], uc_ref[0])
    a = (f * g_ref[0].astype(F32)).astype(BF16)
    gate = mod_ref[0][:, 2 * D_MODEL:]
    o_ref[0] = x_ref[0] + gate * _dot(a, w_ref[...])


def _fourier_out(dft, ucs, g, x, mod, w_bf, is_ctx):
    b, t, _ = x.shape
    tm = _tile(t)
    tok = lambda bb, i: (bb, i, 0)
    return pl.pallas_call(
        _fourier_out_kernel,
        grid=(b, t // tm),
        in_specs=[pl.BlockSpec((tm, 2 * t), lambda bb, i: (i, 0)),
                  pl.BlockSpec((1, 2 * t, D_MODEL), lambda bb, i: (bb, 0, 0)),
                  pl.BlockSpec((1, tm, D_MODEL), tok),
                  pl.BlockSpec((1, tm, D_MODEL), tok),
                  pl.BlockSpec((1, 1, 3 * D_MODEL), _mod_index(is_ctx, b)),
                  pl.BlockSpec((D_MODEL, D_MODEL), lambda bb, i: (0, 0))],
        out_specs=pl.BlockSpec((1, tm, D_MODEL), tok),
        out_shape=jax.ShapeDtypeStruct(x.shape, F32),
        compiler_params=_params("arbitrary", "arbitrary"),
        name="fourier_out_ctx" if is_ctx else "fourier_out",
    )(dft, ucs.reshape(b, 2 * t, D_MODEL), g, x, mod, w_bf)


def _conv_in_kernel(x_ref, mod_ref, nw_ref, w_ref, u_ref, g_ref):
    h = _modnorm(x_ref[0], mod_ref[0], nw_ref[...])
    a = _dot(h, w_ref[:, :D_MODEL])
    gl = _dot(h, w_ref[:, D_MODEL:2 * D_MODEL])
    u_ref[0] = a * _sigmoid(gl)
    g_ref[0] = _silu(_dot(h, w_ref[:, 2 * D_MODEL:])).astype(BF16)


def _conv_in(x, mod, nw, w_bf, is_ctx):
    b, t, _ = x.shape
    tm = _tile(t)
    tok = lambda bb, i: (bb, i, 0)
    const = lambda bb, i: (0, 0)
    return pl.pallas_call(
        _conv_in_kernel,
        grid=(b, t // tm),
        in_specs=[pl.BlockSpec((1, tm, D_MODEL), tok),
                  pl.BlockSpec((1, 1, 3 * D_MODEL), _mod_index(is_ctx, b)),
                  pl.BlockSpec((1, D_MODEL), const),
                  pl.BlockSpec((D_MODEL, 3 * D_MODEL), const)],
        out_specs=[pl.BlockSpec((1, tm, D_MODEL), tok),
                   pl.BlockSpec((1, tm, D_MODEL), tok)],
        out_shape=[jax.ShapeDtypeStruct((b, t, D_MODEL), F32),
                   jax.ShapeDtypeStruct((b, t, D_MODEL), BF16)],
        compiler_params=_params("arbitrary", "arbitrary"),
        name="conv_in_ctx" if is_ctx else "conv_in",
    )(x, mod, nw, w_bf)


def _conv_out_kernel(u_ref, g_ref, x_ref, mod_ref, dw_ref, dwb_ref, lnw_ref, lnb_ref, w_ref,
                     o_ref, upad, conv):
    t = u_ref.shape[1]
    tm = o_ref.shape[1]
    halo = 2 * SUBLANES
    i = pl.program_id(1)

    @pl.when(i == 0)
    def _stage():
        upad[0:halo, :] = jnp.zeros((halo, D_MODEL), F32)
        upad[halo:halo + t, :] = u_ref[0]
        upad[halo + t:, :] = jnp.zeros((halo, D_MODEL), F32)

    def sub_block(j, carry):
        base = pl.multiple_of(i * tm + j * CONV_ROWS, CONV_ROWS)
        win = upad[pl.ds(base, CONV_ROWS + 2 * halo), :]
        acc = jnp.zeros((CONV_ROWS, D_MODEL), F32)
        for k in range(CONV_WIDTH):
            lo = halo - CONV_PAD + k
            acc = acc + win[lo:lo + CONV_ROWS, :] * dw_ref[k:k + 1, :]
        conv[pl.ds(pl.multiple_of(j * CONV_ROWS, CONV_ROWS), CONV_ROWS), :] = acc
        return carry

    lax.fori_loop(0, tm // CONV_ROWS, sub_block, 0)

    c = conv[...] + dwb_ref[...]
    mu = jnp.mean(c, axis=-1, keepdims=True)
    cen = c - mu
    var = jnp.mean(cen * cen, axis=-1, keepdims=True)
    ln = cen * lax.rsqrt(var + EPS) * lnw_ref[...] + lnb_ref[...]
    a = (_silu(ln) * g_ref[0].astype(F32)).astype(BF16)
    gate = mod_ref[0][:, 2 * D_MODEL:]
    o_ref[0] = x_ref[0] + gate * _dot(a, w_ref[...])


def _conv_out(u, g, x, mod, dw_w, dw_b, ln_w, ln_b, w_bf, is_ctx):
    b, t, _ = x.shape
    tm = _tile(t)
    tok = lambda bb, i: (bb, i, 0)
    const = lambda bb, i: (0, 0)
    return pl.pallas_call(
        _conv_out_kernel,
        grid=(b, t // tm),
        in_specs=[pl.BlockSpec((1, t, D_MODEL), lambda bb, i: (bb, 0, 0)),
                  pl.BlockSpec((1, tm, D_MODEL), tok),
                  pl.BlockSpec((1, tm, D_MODEL), tok),
                  pl.BlockSpec((1, 1, 3 * D_MODEL), _mod_index(is_ctx, b)),
                  pl.BlockSpec((CONV_WIDTH, D_MODEL), const),
                  pl.BlockSpec((1, D_MODEL), const),
                  pl.BlockSpec((1, D_MODEL), const),
                  pl.BlockSpec((1, D_MODEL), const),
                  pl.BlockSpec((D_MODEL, D_MODEL), const)],
        out_specs=pl.BlockSpec((1, tm, D_MODEL), tok),
        out_shape=jax.ShapeDtypeStruct(x.shape, F32),
        scratch_shapes=[pltpu.VMEM((t + 4 * SUBLANES, D_MODEL), F32),
                        pltpu.VMEM((tm, D_MODEL), F32)],
        compiler_params=_params("arbitrary", "arbitrary"),
        name="conv_out_ctx" if is_ctx else "conv_out",
    )(u, g, x, mod, dw_w, dw_b.reshape(1, -1), ln_w.reshape(1, -1), ln_b.reshape(1, -1), w_bf)


def _rope_tables(n_tokens):
    pos = jnp.arange(n_tokens)
    pos2 = jnp.stack([pos // GRID_W, pos % GRID_W], axis=-1).astype(F32)
    inv_freq = ROPE_THETA ** (-jnp.arange(ROPE_FREQS, dtype=F32) / ROPE_FREQS)
    ang = pos2[:, :, None] * inv_freq
    cos, sin = jnp.cos(ang), jnp.sin(ang)
    cos64 = jnp.concatenate([cos[:, 0], cos[:, 0], cos[:, 1], cos[:, 1]], axis=-1)
    sin64 = jnp.concatenate([-sin[:, 0], sin[:, 0], -sin[:, 1], sin[:, 1]], axis=-1)
    return jnp.tile(cos64, (1, 2)), jnp.tile(sin64, (1, 2))


def _head_mean_matrix():
    idx = np.arange(MXU_DIM) // HEAD_DIM
    return jnp.asarray((idx[:, None] == idx[None, :]).astype(np.float32) / HEAD_DIM, dtype=BF16)


def kernel(x, c, ctx, c_ctx, l0_ada_w, l0_ada_b, l0_norm_w, l0_w_in, l0_q_norm, l0_k_norm, l0_w_out, l1_ada_w, l1_ada_b, l1_norm_w, l1_w_in, l1_w_out, l2_ada_w, l2_ada_b, l2_norm_w, l2_w_in, l2_dw_w, l2_dw_b, l2_ln_w, l2_ln_b, l2_w_out, l3_ada_w, l3_ada_b, l3_norm_w, l3_w_in, l3_q_norm, l3_k_norm, l3_w_out):
    n_batch, seq, _ = x.shape
    n_ctx = ctx.shape[1]

    rows = -(-(n_batch + 1) // SUBLANES) * SUBLANES
    cc = jnp.concatenate([c, c_ctx[None, :], jnp.zeros((rows - n_batch - 1, D_MODEL), F32)], axis=0)

    def mods(ada_w, ada_b):
        return _ada(cc, ada_w, ada_b).reshape(rows, 1, 3 * D_MODEL)

    cos, sin = _rope_tables(seq)
    cos_ctx = jnp.ones((n_ctx, LANES), F32)
    sin_ctx = jnp.zeros((n_ctx, LANES), F32)
    e_mat = _head_mean_matrix()

    def attention_layer(xl, xc, ada_w, ada_b, nw, w_in, qn, kn, w_out, need_ctx):
        mod = mods(ada_w, ada_b)
        nw2 = nw.reshape(1, -1)
        w_bf = w_in.astype(BF16)
        wo_bf = w_out.astype(BF16)
        qn_t = jnp.tile(qn, N_HEADS).reshape(1, -1)
        kn_t = jnp.tile(kn, N_KV_HEADS).reshape(1, -1)
        q, k, v, g = _attn_in(xl, mod, nw2, w_bf, e_mat, qn_t, kn_t, cos, sin, False)
        qc, kc, vc, gc = _attn_in(xc, mod, nw2, w_bf, e_mat, qn_t, kn_t, cos_ctx, sin_ctx, True)
        a = _attention(q, g, [(kc, vc), (k, v)], "attention")
        xl_new = _proj_out(a, xl, mod, wo_bf, False, "attn_out")
        if need_ctx:
            ac = _attention(qc, gc, [(kc, vc)], "attention_ctx")
            xc = _proj_out(ac, xc, mod, wo_bf, True, "attn_out_ctx")
        return xl_new, xc

    def fourier_layer(xl, xc, ada_w, ada_b, nw, w_in, w_out):
        mod = mods(ada_w, ada_b)
        nw2 = nw.reshape(1, -1)
        w_bf = w_in.astype(BF16)
        wo_bf = w_out.astype(BF16)
        cch, sch = _dft_mats(FOURIER_GROUP_DIM, FOURIER_GROUP_DIM ** -0.5)
        cch, sch = jnp.asarray(cch, dtype=BF16), jnp.asarray(sch, dtype=BF16)
        outs = []
        for t_arr, is_ctx in ((xl, False), (xc, True)):
            t = t_arr.shape[1]
            cp, sp = _dft_mats(t, t ** -0.5)
            dft = jnp.asarray(np.concatenate([cp, -sp], axis=1), dtype=BF16)
            ucs, g = _fourier_in(t_arr, mod, nw2, w_bf, cch, sch, is_ctx)
            outs.append(_fourier_out(dft, ucs, g, t_arr, mod, wo_bf, is_ctx))
        return outs[0], outs[1]

    def conv_layer(xl, xc, ada_w, ada_b, nw, w_in, dw_w, dw_b, ln_w, ln_b, w_out):
        mod = mods(ada_w, ada_b)
        nw2 = nw.reshape(1, -1)
        w_bf = w_in.astype(BF16)
        wo_bf = w_out.astype(BF16)
        outs = []
        for t_arr, is_ctx in ((xl, False), (xc, True)):
            u, g = _conv_in(t_arr, mod, nw2, w_bf, is_ctx)
            outs.append(_conv_out(u, g, t_arr, mod, dw_w, dw_b, ln_w, ln_b, wo_bf, is_ctx))
        return outs[0], outs[1]

    xl, xc = x, ctx
    xl, xc = attention_layer(xl, xc, l0_ada_w, l0_ada_b, l0_norm_w, l0_w_in, l0_q_norm, l0_k_norm,
                             l0_w_out, True)
    xl, xc = fourier_layer(xl, xc, l1_ada_w, l1_ada_b, l1_norm_w, l1_w_in, l1_w_out)
    xl, xc = conv_layer(xl, xc, l2_ada_w, l2_ada_b, l2_norm_w, l2_w_in, l2_dw_w, l2_dw_b,
                        l2_ln_w, l2_ln_b, l2_w_out)
    xl, xc = attention_layer(xl, xc, l3_ada_w, l3_ada_b, l3_norm_w, l3_w_in, l3_q_norm, l3_k_norm,
                             l3_w_out, False)
    return xl
```

```python
import functools
import math

import numpy as np
import jax
import jax.numpy as jnp
from jax import lax
from jax.experimental import pallas as pl
from jax.experimental.pallas import tpu as pltpu

D_MODEL = 1024
N_HEADS = 16
N_KV_HEADS = 4
HEAD_DIM = 64
GQA_GROUP = N_HEADS // N_KV_HEADS
Q_W = N_HEADS * HEAD_DIM
KV_W = N_KV_HEADS * HEAD_DIM
GRID_W = 64
ROPE_FREQS = HEAD_DIM // 4
ROPE_THETA = 10000.0
FOURIER_GROUP_DIM = 256
CONV_WIDTH = 31
CONV_PAD = CONV_WIDTH // 2
EPS = 1e-6

LANES = 128
SUBLANES = 8
MXU_DIM = 256
VMEM_LIMIT = 48 * 1024 * 1024
VMEM_LIMIT_BIG = 56 * 1024 * 1024

F32 = jnp.float32
BF16 = jnp.bfloat16

TOKEN_TILE = 512
SEQ_TILE = 256
Q_TILE = 256
CONV_HALO = 2 * SUBLANES
CONV_BLOCK_ROWS = 64


def _params(*sem, vmem=VMEM_LIMIT):
    return pltpu.CompilerParams(dimension_semantics=sem, vmem_limit_bytes=vmem)


def _dot(a, b):
    return jnp.dot(a, b, preferred_element_type=F32)


def _silu(t):
    return t / (1.0 + jnp.exp(-t))


def _sigmoid(t):
    return 1.0 / (1.0 + jnp.exp(-t))


def _split_bf16(t):
    hi = t.astype(BF16)
    lo = (t - hi.astype(F32)).astype(BF16)
    return hi, lo


def _ada_kernel(cc_ref, w_ref, b_ref, o_ref):
    a_hi, a_lo = _split_bf16(_silu(cc_ref[...]))
    w_hi, w_lo = _split_bf16(w_ref[...])
    o_ref[...] = _dot(a_hi, w_hi) + (_dot(a_hi, w_lo) + _dot(a_lo, w_hi)) + b_ref[...]


def _ada(cc, ada_w, ada_b):
    rows = cc.shape[0]
    n = ada_w.shape[1]
    tn = D_MODEL
    return pl.pallas_call(
        _ada_kernel,
        grid=(n // tn,),
        in_specs=[pl.BlockSpec((rows, D_MODEL), lambda j: (0, 0)),
                  pl.BlockSpec((D_MODEL, tn), lambda j: (0, j)),
                  pl.BlockSpec((1, tn), lambda j: (0, j))],
        out_specs=pl.BlockSpec((rows, tn), lambda j: (0, j)),
        out_shape=jax.ShapeDtypeStruct((rows, n), F32),
        compiler_params=_params("arbitrary"),
        name="ada",
    )(cc, ada_w, ada_b.reshape(1, n))


def _modnorm(x, mod, nw):
    r = lax.rsqrt(jnp.mean(x * x, axis=-1, keepdims=True) + EPS)
    shift = mod[:, :D_MODEL]
    scale = mod[:, D_MODEL:2 * D_MODEL]
    return ((x * r) * (nw * (1.0 + scale)) + shift).astype(BF16)


def _mod_index(is_ctx, n_batch):
    if is_ctx:
        return lambda b, i: (n_batch, 0, 0)
    return lambda b, i: (b, 0, 0)


def _tile(t, rows=TOKEN_TILE):
    return min(rows, t)


def _head_rms(t, e_ref, w):
    outs = []
    for c in range(t.shape[1] // MXU_DIM):
        tc = t[:, c * MXU_DIM:(c + 1) * MXU_DIM]
        ms = _dot((tc * tc).astype(BF16), e_ref[...])
        outs.append(tc * lax.rsqrt(ms + EPS))
    return jnp.concatenate(outs, axis=1) * w


def _rope(t, cos, sin):
    lane = lax.broadcasted_iota(jnp.int32, (t.shape[0], LANES), 1)
    low_half = (lane & ROPE_FREQS) == 0
    outs = []
    for c in range(t.shape[1] // LANES):
        tc = t[:, c * LANES:(c + 1) * LANES]
        partner = jnp.where(low_half,
                            pltpu.roll(tc, LANES - ROPE_FREQS, 1),
                            pltpu.roll(tc, ROPE_FREQS, 1))
        outs.append(tc * cos + partner * sin)
    return jnp.concatenate(outs, axis=1)


def _attn_in_kernel(x_ref, mod_ref, nw_ref, w_ref, e_ref, qn_ref, kn_ref, cos_ref, sin_ref,
                    q_ref, k_ref, v_ref, g_ref):
    h = _modnorm(x_ref[0], mod_ref[0], nw_ref[...])
    cos = cos_ref[...]
    sin = sin_ref[...]
    q = _head_rms(_dot(h, w_ref[:, :Q_W]), e_ref, qn_ref[...])
    q_ref[0] = (_rope(q, cos, sin) * (HEAD_DIM ** -0.5)).astype(BF16)
    k = _head_rms(_dot(h, w_ref[:, Q_W:Q_W + KV_W]), e_ref, kn_ref[...])
    k_ref[0] = _rope(k, cos, sin).astype(BF16)
    g_ref[0] = _silu(_dot(h, w_ref[:, Q_W + 2 * KV_W:])).astype(BF16)
    v_ref[0] = _dot(h, w_ref[:, Q_W + KV_W:Q_W + 2 * KV_W]).astype(BF16)


def _attn_in(x, mod, nw, w_bf, e_mat, qn, kn, cos, sin, is_ctx):
    b, t, _ = x.shape
    tm = _tile(t)
    n_in = w_bf.shape[1]
    tok = lambda bb, i: (bb, i, 0)
    const = lambda bb, i: (0, 0)
    return pl.pallas_call(
        _attn_in_kernel,
        grid=(b, t // tm),
        in_specs=[pl.BlockSpec((1, tm, D_MODEL), tok),
                  pl.BlockSpec((1, 1, 3 * D_MODEL), _mod_index(is_ctx, b)),
                  pl.BlockSpec((1, D_MODEL), const),
                  pl.BlockSpec((D_MODEL, n_in), const),
                  pl.BlockSpec((MXU_DIM, MXU_DIM), const),
                  pl.BlockSpec((1, Q_W), const),
                  pl.BlockSpec((1, KV_W), const),
                  pl.BlockSpec((tm, LANES), lambda bb, i: (i, 0)),
                  pl.BlockSpec((tm, LANES), lambda bb, i: (i, 0))],
        out_specs=[pl.BlockSpec((1, tm, Q_W), tok),
                   pl.BlockSpec((1, tm, KV_W), tok),
                   pl.BlockSpec((1, tm, KV_W), tok),
                   pl.BlockSpec((1, tm, Q_W), tok)],
        out_shape=[jax.ShapeDtypeStruct((b, t, Q_W), BF16),
                   jax.ShapeDtypeStruct((b, t, KV_W), BF16),
                   jax.ShapeDtypeStruct((b, t, KV_W), BF16),
                   jax.ShapeDtypeStruct((b, t, Q_W), BF16)],
        compiler_params=_params("arbitrary", "arbitrary"),
        name="attn_in_ctx" if is_ctx else "attn_in",
    )(x, mod, nw, w_bf, e_mat, qn, kn, cos, sin)


V_ROWS = HEAD_DIM + 16

def _attn_kernel(*refs, lens, n_qtiles, n_units):
    n_src = len(lens)
    q_ref, g_ref = refs[0], refs[1]
    k_refs = refs[2:2 + n_src]
    v_refs = refs[2 + n_src:2 + 2 * n_src]
    o_ref = refs[2 + 2 * n_src]
    k_lo, k_hi, v_t, s_buf = refs[3 + 2 * n_src:]
    n = pl.program_id(0)
    n_keys = sum(lens)
    half = LANES // 2
    units_per_batch = N_KV_HEADS * n_qtiles

    scored = jnp.minimum(n, n_units - 1)
    emitted = jnp.maximum(n - 1, 0)

    @pl.when(n == 0)
    def _fill():
        s_buf[1] = jnp.zeros(s_buf.shape[1:], F32)

    @pl.when(scored % units_per_batch == 0)
    def _stage_k():
        zeros = jnp.zeros((n_keys, half), BF16)
        for gg in range(N_KV_HEADS):
            k_lo[gg, :, half:] = zeros
            k_hi[gg, :, :half] = zeros
            off = 0
            for s in range(n_src):
                kk = k_refs[s][0, :, gg * HEAD_DIM:(gg + 1) * HEAD_DIM]
                k_lo[gg, off:off + lens[s], :half] = kk
                k_hi[gg, off:off + lens[s], half:] = kk
                off += lens[s]

    @pl.when(emitted % units_per_batch == 0)
    def _stage_v():
        row = lax.broadcasted_iota(jnp.int32, (V_ROWS - HEAD_DIM, n_keys), 0)
        ones_row = jnp.where(row == 0, 1.0, 0.0).astype(BF16)
        for gg in range(N_KV_HEADS):
            v_t[gg, HEAD_DIM:, :] = ones_row
        off = 0
        for s in range(n_src):
            vt = v_refs[s][0].astype(F32).T.astype(BF16)
            for gg in range(N_KV_HEADS):
                v_t[gg, :HEAD_DIM, off:off + lens[s]] = vt[gg * HEAD_DIM:(gg + 1) * HEAD_DIM, :]
            off += lens[s]

    def col_max(t):
        while t.shape[0] % (2 * SUBLANES) == 0:
            m = t.shape[0] // 2
            t = jnp.maximum(t[:m, :], t[m:, :])
        return jnp.max(t, axis=0, keepdims=True)

    def step(write_slot, read_slot):
        grp = (scored // n_qtiles) % N_KV_HEADS
        vt_g = v_t[(emitted // n_qtiles) % N_KV_HEADS]
        o_rows = []
        for pair in range(GQA_GROUP // 2):
            q2 = q_ref[0, :, pair * LANES:(pair + 1) * LANES]
            for h, k_st in ((2 * pair, k_lo), (2 * pair + 1, k_hi)):
                s_buf[write_slot, h] = lax.dot_general(k_st[grp], q2, (((1,), (1,)), ((), ())),
                                                       preferred_element_type=F32)
            for h in (2 * pair, 2 * pair + 1):
                s_t = s_buf[read_slot, h]
                p_t = jnp.exp(s_t - col_max(s_t)).astype(BF16)
                o_t = _dot(vt_g, p_t)
                o_rows.append(o_t[:HEAD_DIM, :] / o_t[HEAD_DIM:HEAD_DIM + 1, :])
        o = jnp.concatenate(o_rows, axis=0).T
        o_ref[0] = (o * g_ref[0].astype(F32)).astype(BF16)

    pl.when(n % 2 == 0)(lambda: step(0, 1))
    pl.when(n % 2 == 1)(lambda: step(1, 0))


def _attention(q, g, kvs, name):
    b, t, _ = q.shape
    tq = min(Q_TILE, t)
    n_qtiles = t // tq
    n_units = b * N_KV_HEADS * n_qtiles
    lens = tuple(int(k.shape[1]) for k, _ in kvs)
    n_keys = sum(lens)

    def unit_block(u):
        return (u // (N_KV_HEADS * n_qtiles), u % n_qtiles, (u // n_qtiles) % N_KV_HEADS)

    def unit_batch(u):
        return (u // (N_KV_HEADS * n_qtiles), 0, 0)

    scored = lambda n: jnp.minimum(n, n_units - 1)
    emitted = lambda n: jnp.maximum(n - 1, 0)
    width = GQA_GROUP * HEAD_DIM
    in_specs = [pl.BlockSpec((1, tq, width), lambda n: unit_block(scored(n))),
                pl.BlockSpec((1, tq, width), lambda n: unit_block(emitted(n)))]
    in_specs += [pl.BlockSpec((1, ln, KV_W), lambda n: unit_batch(scored(n))) for ln in lens]
    in_specs += [pl.BlockSpec((1, ln, KV_W), lambda n: unit_batch(emitted(n))) for ln in lens]
    args = [q, g] + [k for k, _ in kvs] + [v for _, v in kvs]
    return pl.pallas_call(
        functools.partial(_attn_kernel, lens=lens, n_qtiles=n_qtiles, n_units=n_units),
        grid=(n_units + 1,),
        in_specs=in_specs,
        out_specs=pl.BlockSpec((1, tq, width), lambda n: unit_block(emitted(n))),
        out_shape=jax.ShapeDtypeStruct((b, t, Q_W), BF16),
        scratch_shapes=[pltpu.VMEM((N_KV_HEADS, n_keys, LANES), BF16),
                        pltpu.VMEM((N_KV_HEADS, n_keys, LANES), BF16),
                        pltpu.VMEM((N_KV_HEADS, V_ROWS, n_keys), BF16),
                        pltpu.VMEM((2, GQA_GROUP, n_keys, tq), F32)],
        compiler_params=_params("arbitrary", vmem=VMEM_LIMIT_BIG),
        name=name,
    )(*args)


def _proj_out_kernel(a_ref, x_ref, mod_ref, w_ref, o_ref):
    gate = mod_ref[0][:, 2 * D_MODEL:]
    o_ref[0] = x_ref[0] + gate * _dot(a_ref[0], w_ref[...])


def _proj_out(a, x, mod, w_bf, is_ctx, name):
    b, t, _ = x.shape
    tm = _tile(t)
    tok = lambda bb, i: (bb, i, 0)
    return pl.pallas_call(
        _proj_out_kernel,
        grid=(b, t // tm),
        in_specs=[pl.BlockSpec((1, tm, D_MODEL), tok),
                  pl.BlockSpec((1, tm, D_MODEL), tok),
                  pl.BlockSpec((1, 1, 3 * D_MODEL), _mod_index(is_ctx, b)),
                  pl.BlockSpec((D_MODEL, D_MODEL), lambda bb, i: (0, 0))],
        out_specs=pl.BlockSpec((1, tm, D_MODEL), tok),
        out_shape=jax.ShapeDtypeStruct(x.shape, F32),
        compiler_params=_params("arbitrary", "arbitrary"),
        name=name,
    )(a, x, mod, w_bf)


def _dft_mats(n, scale):
    idx = np.arange(n, dtype=np.int64)
    ang = 2.0 * np.pi * ((idx[:, None] * idx[None, :]) % n).astype(np.float64) / n
    return np.cos(ang) * scale, np.sin(ang) * scale


def _fourier_in_kernel(x_ref, mod_ref, nw_ref, w_ref, cc_ref, sc_ref, uc_ref, g_ref):
    h = _modnorm(x_ref[0], mod_ref[0], nw_ref[...])
    u = _dot(h, w_ref[:, :D_MODEL]).astype(BF16)
    ucs, uss = [], []
    for c in range(D_MODEL // FOURIER_GROUP_DIM):
        ug = u[:, c * FOURIER_GROUP_DIM:(c + 1) * FOURIER_GROUP_DIM]
        ucs.append(_dot(ug, cc_ref[...]))
        uss.append(_dot(ug, sc_ref[...]))
    uc_ref[0, 0] = jnp.concatenate(ucs, axis=1).astype(BF16)
    uc_ref[0, 1] = jnp.concatenate(uss, axis=1).astype(BF16)
    g_ref[0] = _silu(_dot(h, w_ref[:, D_MODEL:])).astype(BF16)


def _fourier_in(x, mod, nw, w_bf, cmat, smat, is_ctx):
    b, t, _ = x.shape
    tm = _tile(t)
    tok = lambda bb, i: (bb, i, 0)
    const = lambda bb, i: (0, 0)
    return pl.pallas_call(
        _fourier_in_kernel,
        grid=(b, t // tm),
        in_specs=[pl.BlockSpec((1, tm, D_MODEL), tok),
                  pl.BlockSpec((1, 1, 3 * D_MODEL), _mod_index(is_ctx, b)),
                  pl.BlockSpec((1, D_MODEL), const),
                  pl.BlockSpec((D_MODEL, 2 * D_MODEL), const),
                  pl.BlockSpec((FOURIER_GROUP_DIM, FOURIER_GROUP_DIM), const),
                  pl.BlockSpec((FOURIER_GROUP_DIM, FOURIER_GROUP_DIM), const)],
        out_specs=[pl.BlockSpec((1, 2, tm, D_MODEL), lambda bb, i: (bb, 0, i, 0)),
                   pl.BlockSpec((1, tm, D_MODEL), tok)],
        out_shape=[jax.ShapeDtypeStruct((b, 2, t, D_MODEL), BF16),
                   jax.ShapeDtypeStruct((b, t, D_MODEL), BF16)],
        compiler_params=_params("arbitrary", "arbitrary"),
        name="fourier_in_ctx" if is_ctx else "fourier_in",
    )(x, mod, nw, w_bf, cmat, smat)


def _fourier_out_kernel(dft_ref, rev_ref, uc_ref, g_ref, x_ref, mod_ref, w_ref, o_ref, folded):
    n_pos = uc_ref.shape[1] // 2
    half = n_pos // 2
    tm = o_ref.shape[1]
    rt = rev_ref.shape[0]

    @pl.when(pl.program_id(1) == 0)
    def _fold():
        first_row = lax.broadcasted_iota(jnp.int32, (half, D_MODEL), 0) == 0
        for part, sign in ((0, 1.0), (1, -1.0)):
            base = part * n_pos
            tiles = [_dot(rev_ref[...], uc_ref[0, pl.ds(base + n_pos - (j + 1) * rt, rt), :])
                     for j in range(half // rt)]
            rev = jnp.concatenate(tiles, axis=0)
            mirror = jnp.where(first_row, 0.0, pltpu.roll(rev, 1, 0))
            folded[pl.ds(part * half, half), :] = (
                uc_ref[0, pl.ds(base, half), :].astype(F32) + sign * mirror).astype(BF16)

    f = _dot(dft_ref[...], folded[...])
    mid = uc_ref[0, half:half + 2 * SUBLANES, :].astype(F32)[0:1, :] * (n_pos ** -0.5)
    odd = (lax.broadcasted_iota(jnp.int32, (tm, D_MODEL), 0) & 1) == 1
    f = f + jnp.where(odd, -mid, mid)
    a = (f * g_ref[0].astype(F32)).astype(BF16)
    gate = mod_ref[0][:, 2 * D_MODEL:]
    o_ref[0] = x_ref[0] + gate * _dot(a, w_ref[...])


def _fourier_out(dft, rev, ucs, g, x, mod, w_bf, is_ctx):
    b, t, _ = x.shape
    tm = _tile(t, SEQ_TILE)
    tok = lambda bb, i: (bb, i, 0)
    return pl.pallas_call(
        _fourier_out_kernel,
        grid=(b, t // tm),
        in_specs=[pl.BlockSpec((tm, t), lambda bb, i: (i, 0)),
                  pl.BlockSpec(rev.shape, lambda bb, i: (0, 0)),
                  pl.BlockSpec((1, 2 * t, D_MODEL), lambda bb, i: (bb, 0, 0)),
                  pl.BlockSpec((1, tm, D_MODEL), tok),
                  pl.BlockSpec((1, tm, D_MODEL), tok),
                  pl.BlockSpec((1, 1, 3 * D_MODEL), _mod_index(is_ctx, b)),
                  pl.BlockSpec((D_MODEL, D_MODEL), lambda bb, i: (0, 0))],
        out_specs=pl.BlockSpec((1, tm, D_MODEL), tok),
        out_shape=jax.ShapeDtypeStruct(x.shape, F32),
        scratch_shapes=[pltpu.VMEM((t, D_MODEL), BF16)],
        compiler_params=_params("arbitrary", "arbitrary", vmem=VMEM_LIMIT_BIG),
        name="fourier_out_ctx" if is_ctx else "fourier_out",
    )(dft, rev, ucs.reshape(b, 2 * t, D_MODEL), g, x, mod, w_bf)


def _conv_in_kernel(x_ref, mod_ref, nw_ref, w_ref, u_ref, g_ref):
    h = _modnorm(x_ref[0], mod_ref[0], nw_ref[...])
    a = _dot(h, w_ref[:, :D_MODEL])
    gl = _dot(h, w_ref[:, D_MODEL:2 * D_MODEL])
    u_ref[0] = a * _sigmoid(gl)
    g_ref[0] = _silu(_dot(h, w_ref[:, 2 * D_MODEL:])).astype(BF16)


def _conv_in(x, mod, nw, w_bf, is_ctx):
    b, t, _ = x.shape
    tm = _tile(t)
    tok = lambda bb, i: (bb, i, 0)
    const = lambda bb, i: (0, 0)
    return pl.pallas_call(
        _conv_in_kernel,
        grid=(b, t // tm),
        in_specs=[pl.BlockSpec((1, tm, D_MODEL), tok),
                  pl.BlockSpec((1, 1, 3 * D_MODEL), _mod_index(is_ctx, b)),
                  pl.BlockSpec((1, D_MODEL), const),
                  pl.BlockSpec((D_MODEL, 3 * D_MODEL), const)],
        out_specs=[pl.BlockSpec((1, tm, D_MODEL), tok),
                   pl.BlockSpec((1, tm, D_MODEL), tok)],
        out_shape=[jax.ShapeDtypeStruct((b, t, D_MODEL), F32),
                   jax.ShapeDtypeStruct((b, t, D_MODEL), BF16)],
        compiler_params=_params("arbitrary", "arbitrary"),
        name="conv_in_ctx" if is_ctx else "conv_in",
    )(x, mod, nw, w_bf)


def _conv_out_kernel(u_ref, g_ref, x_ref, mod_ref, dw_ref, dwb_ref, lnw_ref, lnb_ref, w_ref,
                     o_ref, upad, phases, conv):
    t = u_ref.shape[1]
    tm = o_ref.shape[1]
    halo = CONV_HALO
    i = pl.program_id(1)

    @pl.when(i == 0)
    def _stage():
        upad[0:halo, :] = jnp.zeros((halo, D_MODEL), F32)
        upad[halo:halo + t, :] = u_ref[0]
        upad[halo + t:, :] = jnp.zeros((halo, D_MODEL), F32)

    win_rows = tm + 2 * halo
    win = upad[pl.ds(pl.multiple_of(i * tm, SUBLANES), win_rows), :]
    phase_rows = phases.shape[2]
    n_cols = D_MODEL // LANES
    for s in range(SUBLANES):
        shifted = win if s == 0 else pltpu.roll(win, win_rows - s, 0)
        for col in range(n_cols):
            phases[s, col] = shifted[:phase_rows, col * LANES:(col + 1) * LANES]

    groups = CONV_BLOCK_ROWS // SUBLANES

    def block(n, carry):
        col = n % n_cols
        base = pl.multiple_of((n // n_cols) * CONV_BLOCK_ROWS, CONV_BLOCK_ROWS)
        acc = jnp.zeros((groups, SUBLANES, LANES), F32)
        for k in range(CONV_WIDTH):
            first = halo - CONV_PAD + k
            s, a = first % SUBLANES, first - first % SUBLANES
            rows = phases[s, col, pl.ds(base + a, CONV_BLOCK_ROWS), :]
            acc = acc + rows.reshape(groups, SUBLANES, LANES) * dw_ref[k, col][None]
        conv[col, pl.ds(base, CONV_BLOCK_ROWS), :] = acc.reshape(CONV_BLOCK_ROWS, LANES)
        return carry

    lax.fori_loop(0, (tm // CONV_BLOCK_ROWS) * n_cols, block, 0, unroll=2)

    c = jnp.concatenate([conv[col] for col in range(n_cols)], axis=1) + dwb_ref[...]
    mu = jnp.mean(c, axis=-1, keepdims=True)
    cen = c - mu
    var = jnp.mean(cen * cen, axis=-1, keepdims=True)
    ln = cen * lax.rsqrt(var + EPS) * lnw_ref[...] + lnb_ref[...]
    a = (_silu(ln) * g_ref[0].astype(F32)).astype(BF16)
    gate = mod_ref[0][:, 2 * D_MODEL:]
    o_ref[0] = x_ref[0] + gate * _dot(a, w_ref[...])


def _conv_out(u, g, x, mod, dw_w, dw_b, ln_w, ln_b, w_bf, is_ctx):
    b, t, _ = x.shape
    tm = _tile(t, SEQ_TILE)
    dw_tiles = jnp.broadcast_to(dw_w.reshape(CONV_WIDTH, D_MODEL // LANES, 1, LANES),
                                (CONV_WIDTH, D_MODEL // LANES, SUBLANES, LANES))
    tok = lambda bb, i: (bb, i, 0)
    const = lambda bb, i: (0, 0)
    return pl.pallas_call(
        _conv_out_kernel,
        grid=(b, t // tm),
        in_specs=[pl.BlockSpec((1, t, D_MODEL), lambda bb, i: (bb, 0, 0)),
                  pl.BlockSpec((1, tm, D_MODEL), tok),
                  pl.BlockSpec((1, tm, D_MODEL), tok),
                  pl.BlockSpec((1, 1, 3 * D_MODEL), _mod_index(is_ctx, b)),
                  pl.BlockSpec((CONV_WIDTH, D_MODEL // LANES, SUBLANES, LANES),
                               lambda bb, i: (0, 0, 0, 0)),
                  pl.BlockSpec((1, D_MODEL), const),
                  pl.BlockSpec((1, D_MODEL), const),
                  pl.BlockSpec((1, D_MODEL), const),
                  pl.BlockSpec((D_MODEL, D_MODEL), const)],
        out_specs=pl.BlockSpec((1, tm, D_MODEL), tok),
        out_shape=jax.ShapeDtypeStruct(x.shape, F32),
        scratch_shapes=[pltpu.VMEM((t + 2 * CONV_HALO, D_MODEL), F32),
                        pltpu.VMEM((SUBLANES, D_MODEL // LANES, tm + CONV_HALO + SUBLANES, LANES), F32),
                        pltpu.VMEM((D_MODEL // LANES, tm, LANES), F32)],
        compiler_params=_params("arbitrary", "arbitrary", vmem=VMEM_LIMIT_BIG),
        name="conv_out_ctx" if is_ctx else "conv_out",
    )(u, g, x, mod, dw_tiles, dw_b.reshape(1, -1), ln_w.reshape(1, -1), ln_b.reshape(1, -1), w_bf)


def _rope_tables(n_tokens):
    pos = jnp.arange(n_tokens)
    pos2 = jnp.stack([pos // GRID_W, pos % GRID_W], axis=-1).astype(F32)
    inv_freq = ROPE_THETA ** (-jnp.arange(ROPE_FREQS, dtype=F32) / ROPE_FREQS)
    ang = pos2[:, :, None] * inv_freq
    cos, sin = jnp.cos(ang), jnp.sin(ang)
    cos64 = jnp.concatenate([cos[:, 0], cos[:, 0], cos[:, 1], cos[:, 1]], axis=-1)
    sin64 = jnp.concatenate([-sin[:, 0], sin[:, 0], -sin[:, 1], sin[:, 1]], axis=-1)
    return jnp.tile(cos64, (1, 2)), jnp.tile(sin64, (1, 2))


def _head_mean_matrix():
    idx = np.arange(MXU_DIM) // HEAD_DIM
    return jnp.asarray((idx[:, None] == idx[None, :]).astype(np.float32) / HEAD_DIM, dtype=BF16)


def kernel(x, c, ctx, c_ctx, l0_ada_w, l0_ada_b, l0_norm_w, l0_w_in, l0_q_norm, l0_k_norm, l0_w_out, l1_ada_w, l1_ada_b, l1_norm_w, l1_w_in, l1_w_out, l2_ada_w, l2_ada_b, l2_norm_w, l2_w_in, l2_dw_w, l2_dw_b, l2_ln_w, l2_ln_b, l2_w_out, l3_ada_w, l3_ada_b, l3_norm_w, l3_w_in, l3_q_norm, l3_k_norm, l3_w_out):
    n_batch, seq, _ = x.shape
    n_ctx = ctx.shape[1]

    rows = -(-(n_batch + 1) // SUBLANES) * SUBLANES
    cc = jnp.concatenate([c, c_ctx[None, :], jnp.zeros((rows - n_batch - 1, D_MODEL), F32)], axis=0)

    def mods(ada_w, ada_b):
        return _ada(cc, ada_w, ada_b).reshape(rows, 1, 3 * D_MODEL)

    cos, sin = _rope_tables(seq)
    cos_ctx = jnp.ones((n_ctx, LANES), F32)
    sin_ctx = jnp.zeros((n_ctx, LANES), F32)
    e_mat = _head_mean_matrix()

    def attention_layer(xl, xc, ada_w, ada_b, nw, w_in, qn, kn, w_out, need_ctx):
        mod = mods(ada_w, ada_b)
        nw2 = nw.reshape(1, -1)
        w_bf = w_in.astype(BF16)
        wo_bf = w_out.astype(BF16)
        qn_t = jnp.tile(qn, N_HEADS).reshape(1, -1)
        kn_t = jnp.tile(kn, N_KV_HEADS).reshape(1, -1)
        q, k, v, g = _attn_in(xl, mod, nw2, w_bf, e_mat, qn_t, kn_t, cos, sin, False)
        qc, kc, vc, gc = _attn_in(xc, mod, nw2, w_bf, e_mat, qn_t, kn_t, cos_ctx, sin_ctx, True)
        a = _attention(q, g, [(kc, vc), (k, v)], "attention")
        xl_new = _proj_out(a, xl, mod, wo_bf, False, "attn_out")
        if need_ctx:
            ac = _attention(qc, gc, [(kc, vc)], "attention_ctx")
            xc = _proj_out(ac, xc, mod, wo_bf, True, "attn_out_ctx")
        return xl_new, xc

    def fourier_layer(xl, xc, ada_w, ada_b, nw, w_in, w_out):
        mod = mods(ada_w, ada_b)
        nw2 = nw.reshape(1, -1)
        w_bf = w_in.astype(BF16)
        wo_bf = w_out.astype(BF16)
        cch, sch = _dft_mats(FOURIER_GROUP_DIM, FOURIER_GROUP_DIM ** -0.5)
        cch, sch = jnp.asarray(cch, dtype=BF16), jnp.asarray(sch, dtype=BF16)
        outs = []
        for t_arr, is_ctx in ((xl, False), (xc, True)):
            t = t_arr.shape[1]
            cp, sp = _dft_mats(t, t ** -0.5)
            dft = jnp.asarray(np.concatenate([cp[:, :t // 2], -sp[:, :t // 2]], axis=1), dtype=BF16)
            rt = min(MXU_DIM, t // 2)
            rev = jnp.asarray(np.eye(rt)[::-1], dtype=BF16)
            ucs, g = _fourier_in(t_arr, mod, nw2, w_bf, cch, sch, is_ctx)
            outs.append(_fourier_out(dft, rev, ucs, g, t_arr, mod, wo_bf, is_ctx))
        return outs[0], outs[1]

    def conv_layer(xl, xc, ada_w, ada_b, nw, w_in, dw_w, dw_b, ln_w, ln_b, w_out):
        mod = mods(ada_w, ada_b)
        nw2 = nw.reshape(1, -1)
        w_bf = w_in.astype(BF16)
        wo_bf = w_out.astype(BF16)
        outs = []
        for t_arr, is_ctx in ((xl, False), (xc, True)):
            u, g = _conv_in(t_arr, mod, nw2, w_bf, is_ctx)
            outs.append(_conv_out(u, g, t_arr, mod, dw_w, dw_b, ln_w, ln_b, wo_bf, is_ctx))
        return outs[0], outs[1]

    xl, xc = x, ctx
    xl, xc = attention_layer(xl, xc, l0_ada_w, l0_ada_b, l0_norm_w, l0_w_in, l0_q_norm, l0_k_norm,
                             l0_w_out, True)
    xl, xc = fourier_layer(xl, xc, l1_ada_w, l1_ada_b, l1_norm_w, l1_w_in, l1_w_out)
    xl, xc = conv_layer(xl, xc, l2_ada_w, l2_ada_b, l2_norm_w, l2_w_in, l2_dw_w, l2_dw_b,
                        l2_ln_w, l2_ln_b, l2_w_out)
    xl, xc = attention_layer(xl, xc, l3_ada_w, l3_ada_b, l3_norm_w, l3_w_in, l3_q_norm, l3_k_norm,
                             l3_w_out, False)
    return xl
```

```python
import functools
import math

import numpy as np
import jax
import jax.numpy as jnp
from jax import lax
from jax.experimental import pallas as pl
from jax.experimental.pallas import tpu as pltpu

D_MODEL = 1024
N_HEADS = 16
N_KV_HEADS = 4
HEAD_DIM = 64
GQA_GROUP = N_HEADS // N_KV_HEADS
Q_W = N_HEADS * HEAD_DIM
KV_W = N_KV_HEADS * HEAD_DIM
GRID_W = 64
ROPE_FREQS = HEAD_DIM // 4
ROPE_THETA = 10000.0
FOURIER_GROUP_DIM = 256
CONV_WIDTH = 31
CONV_PAD = CONV_WIDTH // 2
EPS = 1e-6

LANES = 128
SUBLANES = 8
MXU_DIM = 256
VMEM_LIMIT = 48 * 1024 * 1024
VMEM_LIMIT_BIG = 56 * 1024 * 1024

F32 = jnp.float32
BF16 = jnp.bfloat16

TOKEN_TILE = 512
SEQ_TILE = 256
Q_TILE = 256
CONV_HALO = 2 * SUBLANES
CONV_BLOCK_ROWS = 64


def _params(*sem, vmem=VMEM_LIMIT):
    return pltpu.CompilerParams(dimension_semantics=sem, vmem_limit_bytes=vmem)


def _dot(a, b):
    return jnp.dot(a, b, preferred_element_type=F32)


def _silu(t):
    return t / (1.0 + jnp.exp(-t))


def _sigmoid(t):
    return 1.0 / (1.0 + jnp.exp(-t))


def _split_bf16(t):
    hi = t.astype(BF16)
    lo = (t - hi.astype(F32)).astype(BF16)
    return hi, lo


def _ada_kernel(cc_ref, w_ref, b_ref, o_ref):
    a_hi, a_lo = _split_bf16(_silu(cc_ref[...]))
    w_hi, w_lo = _split_bf16(w_ref[...])
    o_ref[...] = _dot(a_hi, w_hi) + (_dot(a_hi, w_lo) + _dot(a_lo, w_hi)) + b_ref[...]


def _ada(cc, ada_w, ada_b):
    rows = cc.shape[0]
    n = ada_w.shape[1]
    tn = D_MODEL
    return pl.pallas_call(
        _ada_kernel,
        grid=(n // tn,),
        in_specs=[pl.BlockSpec((rows, D_MODEL), lambda j: (0, 0)),
                  pl.BlockSpec((D_MODEL, tn), lambda j: (0, j)),
                  pl.BlockSpec((1, tn), lambda j: (0, j))],
        out_specs=pl.BlockSpec((rows, tn), lambda j: (0, j)),
        out_shape=jax.ShapeDtypeStruct((rows, n), F32),
        compiler_params=_params("arbitrary"),
        name="ada",
    )(cc, ada_w, ada_b.reshape(1, n))


def _modnorm(x, mod, nw):
    r = lax.rsqrt(jnp.mean(x * x, axis=-1, keepdims=True) + EPS)
    shift = mod[:, :D_MODEL]
    scale = mod[:, D_MODEL:2 * D_MODEL]
    return ((x * r) * (nw * (1.0 + scale)) + shift).astype(BF16)


def _mod_index(is_ctx, n_batch):
    if is_ctx:
        return lambda b, i: (n_batch, 0, 0)
    return lambda b, i: (b, 0, 0)


def _tile(t, rows=TOKEN_TILE):
    return min(rows, t)


def _head_rms(t, e_ref, w):
    outs = []
    for c in range(t.shape[1] // MXU_DIM):
        tc = t[:, c * MXU_DIM:(c + 1) * MXU_DIM]
        ms = _dot((tc * tc).astype(BF16), e_ref[...])
        outs.append(tc * lax.rsqrt(ms + EPS))
    return jnp.concatenate(outs, axis=1) * w


def _rope(t, cos, sin):
    lane = lax.broadcasted_iota(jnp.int32, (t.shape[0], LANES), 1)
    low_half = (lane & ROPE_FREQS) == 0
    outs = []
    for c in range(t.shape[1] // LANES):
        tc = t[:, c * LANES:(c + 1) * LANES]
        partner = jnp.where(low_half,
                            pltpu.roll(tc, LANES - ROPE_FREQS, 1),
                            pltpu.roll(tc, ROPE_FREQS, 1))
        outs.append(tc * cos + partner * sin)
    return jnp.concatenate(outs, axis=1)


def _attn_in_kernel(x_ref, mod_ref, nw_ref, w_ref, e_ref, qn_ref, kn_ref, cos_ref, sin_ref,
                    *out_refs, keys_only):
    h = _modnorm(x_ref[0], mod_ref[0], nw_ref[...])
    cos = cos_ref[...]
    sin = sin_ref[...]
    if keys_only:
        k_ref, v_ref = out_refs
    else:
        q_ref, k_ref, v_ref, g_ref = out_refs
        q = _head_rms(_dot(h, w_ref[:, :Q_W]), e_ref, qn_ref[...])
        q_ref[0] = (_rope(q, cos, sin) * (HEAD_DIM ** -0.5)).astype(BF16)
    k = _head_rms(_dot(h, w_ref[:, Q_W:Q_W + KV_W]), e_ref, kn_ref[...])
    k_ref[0] = _rope(k, cos, sin).astype(BF16)
    if not keys_only:
        g_ref[0] = _silu(_dot(h, w_ref[:, Q_W + 2 * KV_W:])).astype(BF16)
    v_ref[0] = _dot(h, w_ref[:, Q_W + KV_W:Q_W + 2 * KV_W]).astype(BF16)


def _attn_in(x, mod, nw, w_bf, e_mat, qn, kn, cos, sin, is_ctx, keys_only=False):
    b, t, _ = x.shape
    tm = _tile(t)
    n_in = w_bf.shape[1]
    tok = lambda bb, i: (bb, i, 0)
    const = lambda bb, i: (0, 0)
    widths = (KV_W, KV_W) if keys_only else (Q_W, KV_W, KV_W, Q_W)
    return pl.pallas_call(
        functools.partial(_attn_in_kernel, keys_only=keys_only),
        grid=(b, t // tm),
        in_specs=[pl.BlockSpec((1, tm, D_MODEL), tok),
                  pl.BlockSpec((1, 1, 3 * D_MODEL), _mod_index(is_ctx, b)),
                  pl.BlockSpec((1, D_MODEL), const),
                  pl.BlockSpec((D_MODEL, n_in), const),
                  pl.BlockSpec((MXU_DIM, MXU_DIM), const),
                  pl.BlockSpec((1, Q_W), const),
                  pl.BlockSpec((1, KV_W), const),
                  pl.BlockSpec((tm, LANES), lambda bb, i: (i, 0)),
                  pl.BlockSpec((tm, LANES), lambda bb, i: (i, 0))],
        out_specs=[pl.BlockSpec((1, tm, w), tok) for w in widths],
        out_shape=[jax.ShapeDtypeStruct((b, t, w), BF16) for w in widths],
        compiler_params=_params("arbitrary", "arbitrary"),
        name="attn_in_ctx" if is_ctx else "attn_in",
    )(x, mod, nw, w_bf, e_mat, qn, kn, cos, sin)


V_ROWS = HEAD_DIM + 16

def _attn_kernel(*refs, lens, n_qtiles, n_units):
    n_src = len(lens)
    q_ref, g_ref = refs[0], refs[1]
    k_refs = refs[2:2 + n_src]
    v_refs = refs[2 + n_src:2 + 2 * n_src]
    o_ref = refs[2 + 2 * n_src]
    k_lo, k_hi, v_t, s_buf, m_buf, o_buf = refs[3 + 2 * n_src:]
    n = pl.program_id(0)
    n_keys = sum(lens)
    half = LANES // 2
    units_per_batch = N_KV_HEADS * n_qtiles

    scored = jnp.minimum(n, n_units - 1)
    emitted = jnp.clip(n - 1, 0, n_units - 1)

    @pl.when(n == 0)
    def _fill():
        s_buf[1] = jnp.zeros(s_buf.shape[1:], F32)
        m_buf[1] = jnp.zeros(m_buf.shape[1:], F32)
        o_buf[1] = jnp.ones(o_buf.shape[1:], F32)

    @pl.when(scored % units_per_batch == 0)
    def _stage_k():
        zeros = jnp.zeros((n_keys, half), BF16)
        for gg in range(N_KV_HEADS):
            k_lo[gg, :, half:] = zeros
            k_hi[gg, :, :half] = zeros
            off = 0
            for s in range(n_src):
                kk = k_refs[s][0, :, gg * HEAD_DIM:(gg + 1) * HEAD_DIM]
                k_lo[gg, off:off + lens[s], :half] = kk
                k_hi[gg, off:off + lens[s], half:] = kk
                off += lens[s]

    @pl.when(emitted % units_per_batch == 0)
    def _stage_v():
        row = lax.broadcasted_iota(jnp.int32, (V_ROWS - HEAD_DIM, n_keys), 0)
        ones_row = jnp.where(row == 0, 1.0, 0.0).astype(BF16)
        for gg in range(N_KV_HEADS):
            v_t[gg, HEAD_DIM:, :] = ones_row
        off = 0
        for s in range(n_src):
            vt = v_refs[s][0].astype(F32).T.astype(BF16)
            for gg in range(N_KV_HEADS):
                v_t[gg, :HEAD_DIM, off:off + lens[s]] = vt[gg * HEAD_DIM:(gg + 1) * HEAD_DIM, :]
            off += lens[s]

    def col_max(t):
        while t.shape[0] % (2 * SUBLANES) == 0:
            m = t.shape[0] // 2
            t = jnp.maximum(t[:m, :], t[m:, :])
        return jnp.max(t, axis=0, keepdims=True)

    def step(write_slot, read_slot):
        grp = (scored // n_qtiles) % N_KV_HEADS
        vt_g = v_t[(emitted // n_qtiles) % N_KV_HEADS]
        for pair in range(GQA_GROUP // 2):
            q2 = q_ref[0, :, pair * LANES:(pair + 1) * LANES]
            for h, k_st in ((2 * pair, k_lo), (2 * pair + 1, k_hi)):
                s_t = lax.dot_general(k_st[grp], q2, (((1,), (1,)), ((), ())),
                                      preferred_element_type=F32)
                s_buf[write_slot, h] = s_t
                m_buf[write_slot, h] = col_max(s_t)
            for h in (2 * pair, 2 * pair + 1):
                p_t = jnp.exp(s_buf[read_slot, h] - m_buf[read_slot, h]).astype(BF16)
                o_buf[write_slot, h] = _dot(vt_g, p_t)
        o_rows = [o_buf[read_slot, h, :HEAD_DIM, :] / o_buf[read_slot, h, HEAD_DIM:HEAD_DIM + 1, :]
                  for h in range(GQA_GROUP)]
        o = jnp.concatenate(o_rows, axis=0).T
        o_ref[0] = (o * g_ref[0].astype(F32)).astype(BF16)

    pl.when(n % 2 == 0)(lambda: step(0, 1))
    pl.when(n % 2 == 1)(lambda: step(1, 0))


def _attention(q, g, kvs, name):
    b, t, _ = q.shape
    tq = min(Q_TILE, t)
    n_qtiles = t // tq
    n_units = b * N_KV_HEADS * n_qtiles
    lens = tuple(int(k.shape[1]) for k, _ in kvs)
    n_keys = sum(lens)

    def unit_block(u):
        return (u // (N_KV_HEADS * n_qtiles), u % n_qtiles, (u // n_qtiles) % N_KV_HEADS)

    def unit_batch(u):
        return (u // (N_KV_HEADS * n_qtiles), 0, 0)

    scored = lambda n: jnp.minimum(n, n_units - 1)
    emitted = lambda n: jnp.clip(n - 1, 0, n_units - 1)
    stored = lambda n: jnp.clip(n - 2, 0, n_units - 1)
    width = GQA_GROUP * HEAD_DIM
    in_specs = [pl.BlockSpec((1, tq, width), lambda n: unit_block(scored(n))),
                pl.BlockSpec((1, tq, width), lambda n: unit_block(stored(n)))]
    in_specs += [pl.BlockSpec((1, ln, KV_W), lambda n: unit_batch(scored(n))) for ln in lens]
    in_specs += [pl.BlockSpec((1, ln, KV_W), lambda n: unit_batch(emitted(n))) for ln in lens]
    args = [q, g] + [k for k, _ in kvs] + [v for _, v in kvs]
    return pl.pallas_call(
        functools.partial(_attn_kernel, lens=lens, n_qtiles=n_qtiles, n_units=n_units),
        grid=(n_units + 2,),
        in_specs=in_specs,
        out_specs=pl.BlockSpec((1, tq, width), lambda n: unit_block(stored(n))),
        out_shape=jax.ShapeDtypeStruct((b, t, Q_W), BF16),
        scratch_shapes=[pltpu.VMEM((N_KV_HEADS, n_keys, LANES), BF16),
                        pltpu.VMEM((N_KV_HEADS, n_keys, LANES), BF16),
                        pltpu.VMEM((N_KV_HEADS, V_ROWS, n_keys), BF16),
                        pltpu.VMEM((2, GQA_GROUP, n_keys, tq), F32),
                        pltpu.VMEM((2, GQA_GROUP, 1, tq), F32),
                        pltpu.VMEM((2, GQA_GROUP, V_ROWS, tq), F32)],
        compiler_params=_params("arbitrary", vmem=VMEM_LIMIT_BIG),
        name=name,
    )(*args)


def _proj_out_kernel(a_ref, x_ref, mod_ref, w_ref, o_ref):
    gate = mod_ref[0][:, 2 * D_MODEL:]
    o_ref[0] = x_ref[0] + gate * _dot(a_ref[0], w_ref[...])


def _proj_out(a, x, mod, w_bf, is_ctx, name):
    b, t, _ = x.shape
    tm = _tile(t)
    tok = lambda bb, i: (bb, i, 0)
    return pl.pallas_call(
        _proj_out_kernel,
        grid=(b, t // tm),
        in_specs=[pl.BlockSpec((1, tm, D_MODEL), tok),
                  pl.BlockSpec((1, tm, D_MODEL), tok),
                  pl.BlockSpec((1, 1, 3 * D_MODEL), _mod_index(is_ctx, b)),
                  pl.BlockSpec((D_MODEL, D_MODEL), lambda bb, i: (0, 0))],
        out_specs=pl.BlockSpec((1, tm, D_MODEL), tok),
        out_shape=jax.ShapeDtypeStruct(x.shape, F32),
        compiler_params=_params("arbitrary", "arbitrary"),
        name=name,
    )(a, x, mod, w_bf)


def _dft_mats(n, scale):
    idx = np.arange(n, dtype=np.int64)
    ang = 2.0 * np.pi * ((idx[:, None] * idx[None, :]) % n).astype(np.float64) / n
    return np.cos(ang) * scale, np.sin(ang) * scale


def _fourier_in_kernel(x_ref, mod_ref, nw_ref, w_ref, cc_ref, sc_ref, uc_ref, g_ref):
    h = _modnorm(x_ref[0], mod_ref[0], nw_ref[...])
    u = _dot(h, w_ref[:, :D_MODEL]).astype(BF16)
    ucs, uss = [], []
    for c in range(D_MODEL // FOURIER_GROUP_DIM):
        ug = u[:, c * FOURIER_GROUP_DIM:(c + 1) * FOURIER_GROUP_DIM]
        ucs.append(_dot(ug, cc_ref[...]))
        uss.append(_dot(ug, sc_ref[...]))
    uc_ref[0, 0] = jnp.concatenate(ucs, axis=1).astype(BF16)
    uc_ref[0, 1] = jnp.concatenate(uss, axis=1).astype(BF16)
    g_ref[0] = _silu(_dot(h, w_ref[:, D_MODEL:])).astype(BF16)


def _fourier_in(x, mod, nw, w_bf, cmat, smat, is_ctx):
    b, t, _ = x.shape
    tm = _tile(t)
    tok = lambda bb, i: (bb, i, 0)
    const = lambda bb, i: (0, 0)
    return pl.pallas_call(
        _fourier_in_kernel,
        grid=(b, t // tm),
        in_specs=[pl.BlockSpec((1, tm, D_MODEL), tok),
                  pl.BlockSpec((1, 1, 3 * D_MODEL), _mod_index(is_ctx, b)),
                  pl.BlockSpec((1, D_MODEL), const),
                  pl.BlockSpec((D_MODEL, 2 * D_MODEL), const),
                  pl.BlockSpec((FOURIER_GROUP_DIM, FOURIER_GROUP_DIM), const),
                  pl.BlockSpec((FOURIER_GROUP_DIM, FOURIER_GROUP_DIM), const)],
        out_specs=[pl.BlockSpec((1, 2, tm, D_MODEL), lambda bb, i: (bb, 0, i, 0)),
                   pl.BlockSpec((1, tm, D_MODEL), tok)],
        out_shape=[jax.ShapeDtypeStruct((b, 2, t, D_MODEL), BF16),
                   jax.ShapeDtypeStruct((b, t, D_MODEL), BF16)],
        compiler_params=_params("arbitrary", "arbitrary"),
        name="fourier_in_ctx" if is_ctx else "fourier_in",
    )(x, mod, nw, w_bf, cmat, smat)


def _fourier_out_kernel(dft_ref, rev_ref, uc_ref, g_ref, x_ref, mod_ref, w_ref, o_ref, folded):
    n_pos = uc_ref.shape[1] // 2
    half = n_pos // 2
    tm = o_ref.shape[1]
    rt = rev_ref.shape[0]

    @pl.when(pl.program_id(1) == 0)
    def _fold():
        first_row = lax.broadcasted_iota(jnp.int32, (half, D_MODEL), 0) == 0
        for part, sign in ((0, 1.0), (1, -1.0)):
            base = part * n_pos
            tiles = [_dot(rev_ref[...], uc_ref[0, pl.ds(base + n_pos - (j + 1) * rt, rt), :])
                     for j in range(half // rt)]
            rev = jnp.concatenate(tiles, axis=0)
            mirror = jnp.where(first_row, 0.0, pltpu.roll(rev, 1, 0))
            folded[pl.ds(part * half, half), :] = (
                uc_ref[0, pl.ds(base, half), :].astype(F32) + sign * mirror).astype(BF16)

    f = _dot(dft_ref[...], folded[...])
    mid = uc_ref[0, half:half + 2 * SUBLANES, :].astype(F32)[0:1, :] * (n_pos ** -0.5)
    odd = (lax.broadcasted_iota(jnp.int32, (tm, D_MODEL), 0) & 1) == 1
    f = f + jnp.where(odd, -mid, mid)
    a = (f * g_ref[0].astype(F32)).astype(BF16)
    gate = mod_ref[0][:, 2 * D_MODEL:]
    o_ref[0] = x_ref[0] + gate * _dot(a, w_ref[...])


def _fourier_out(dft, rev, ucs, g, x, mod, w_bf, is_ctx):
    b, t, _ = x.shape
    tm = _tile(t, SEQ_TILE)
    tok = lambda bb, i: (bb, i, 0)
    return pl.pallas_call(
        _fourier_out_kernel,
        grid=(b, t // tm),
        in_specs=[pl.BlockSpec((tm, t), lambda bb, i: (i, 0)),
                  pl.BlockSpec(rev.shape, lambda bb, i: (0, 0)),
                  pl.BlockSpec((1, 2 * t, D_MODEL), lambda bb, i: (bb, 0, 0)),
                  pl.BlockSpec((1, tm, D_MODEL), tok),
                  pl.BlockSpec((1, tm, D_MODEL), tok),
                  pl.BlockSpec((1, 1, 3 * D_MODEL), _mod_index(is_ctx, b)),
                  pl.BlockSpec((D_MODEL, D_MODEL), lambda bb, i: (0, 0))],
        out_specs=pl.BlockSpec((1, tm, D_MODEL), tok),
        out_shape=jax.ShapeDtypeStruct(x.shape, F32),
        scratch_shapes=[pltpu.VMEM((t, D_MODEL), BF16)],
        compiler_params=_params("arbitrary", "arbitrary", vmem=VMEM_LIMIT_BIG),
        name="fourier_out_ctx" if is_ctx else "fourier_out",
    )(dft, rev, ucs.reshape(b, 2 * t, D_MODEL), g, x, mod, w_bf)


def _conv_in_kernel(x_ref, mod_ref, nw_ref, w_ref, u_ref, g_ref):
    h = _modnorm(x_ref[0], mod_ref[0], nw_ref[...])
    a = _dot(h, w_ref[:, :D_MODEL])
    gl = _dot(h, w_ref[:, D_MODEL:2 * D_MODEL])
    u_ref[0] = a * _sigmoid(gl)
    g_ref[0] = _silu(_dot(h, w_ref[:, 2 * D_MODEL:])).astype(BF16)


def _conv_in(x, mod, nw, w_bf, is_ctx):
    b, t, _ = x.shape
    tm = _tile(t)
    tok = lambda bb, i: (bb, i, 0)
    const = lambda bb, i: (0, 0)
    return pl.pallas_call(
        _conv_in_kernel,
        grid=(b, t // tm),
        in_specs=[pl.BlockSpec((1, tm, D_MODEL), tok),
                  pl.BlockSpec((1, 1, 3 * D_MODEL), _mod_index(is_ctx, b)),
                  pl.BlockSpec((1, D_MODEL), const),
                  pl.BlockSpec((D_MODEL, 3 * D_MODEL), const)],
        out_specs=[pl.BlockSpec((1, tm, D_MODEL), tok),
                   pl.BlockSpec((1, tm, D_MODEL), tok)],
        out_shape=[jax.ShapeDtypeStruct((b, t, D_MODEL), F32),
                   jax.ShapeDtypeStruct((b, t, D_MODEL), BF16)],
        compiler_params=_params("arbitrary", "arbitrary"),
        name="conv_in_ctx" if is_ctx else "conv_in",
    )(x, mod, nw, w_bf)


def _conv_out_kernel(u_ref, g_ref, x_ref, mod_ref, dw_ref, dwb_ref, lnw_ref, lnb_ref, w_ref,
                     o_ref, upad, conv):
    t = u_ref.shape[1]
    tm = o_ref.shape[1]
    halo = CONV_HALO
    n_cols = D_MODEL // LANES
    i = pl.program_id(1)

    @pl.when(i == 0)
    def _stage():
        for col in range(n_cols):
            upad[col, 0:halo, :] = jnp.zeros((halo, LANES), F32)
            upad[col, halo:halo + t, :] = u_ref[0, :, col * LANES:(col + 1) * LANES]
            upad[col, halo + t:, :] = jnp.zeros((halo, LANES), F32)

    groups = CONV_BLOCK_ROWS // SUBLANES
    row0 = pl.multiple_of(i * tm, SUBLANES)

    def block(n, carry):
        col = n % n_cols
        base = pl.multiple_of((n // n_cols) * CONV_BLOCK_ROWS, CONV_BLOCK_ROWS)
        acc = jnp.zeros((groups, SUBLANES, LANES), F32)
        for k in range(CONV_WIDTH):
            rows = upad[col, pl.ds(row0 + base + (halo - CONV_PAD + k), CONV_BLOCK_ROWS), :]
            acc = acc + rows.reshape(groups, SUBLANES, LANES) * dw_ref[k, col][None]
        conv[col, pl.ds(base, CONV_BLOCK_ROWS), :] = acc.reshape(CONV_BLOCK_ROWS, LANES)
        return carry

    lax.fori_loop(0, (tm // CONV_BLOCK_ROWS) * n_cols, block, 0, unroll=2)

    c = jnp.concatenate([conv[col] for col in range(n_cols)], axis=1) + dwb_ref[...]
    mu = jnp.mean(c, axis=-1, keepdims=True)
    cen = c - mu
    var = jnp.mean(cen * cen, axis=-1, keepdims=True)
    ln = cen * lax.rsqrt(var + EPS) * lnw_ref[...] + lnb_ref[...]
    a = (_silu(ln) * g_ref[0].astype(F32)).astype(BF16)
    gate = mod_ref[0][:, 2 * D_MODEL:]
    o_ref[0] = x_ref[0] + gate * _dot(a, w_ref[...])


def _conv_out(u, g, x, mod, dw_w, dw_b, ln_w, ln_b, w_bf, is_ctx):
    b, t, _ = x.shape
    tm = _tile(t, SEQ_TILE)
    dw_tiles = jnp.broadcast_to(dw_w.reshape(CONV_WIDTH, D_MODEL // LANES, 1, LANES),
                                (CONV_WIDTH, D_MODEL // LANES, SUBLANES, LANES))
    tok = lambda bb, i: (bb, i, 0)
    const = lambda bb, i: (0, 0)
    return pl.pallas_call(
        _conv_out_kernel,
        grid=(b, t // tm),
        in_specs=[pl.BlockSpec((1, t, D_MODEL), lambda bb, i: (bb, 0, 0)),
                  pl.BlockSpec((1, tm, D_MODEL), tok),
                  pl.BlockSpec((1, tm, D_MODEL), tok),
                  pl.BlockSpec((1, 1, 3 * D_MODEL), _mod_index(is_ctx, b)),
                  pl.BlockSpec((CONV_WIDTH, D_MODEL // LANES, SUBLANES, LANES),
                               lambda bb, i: (0, 0, 0, 0)),
                  pl.BlockSpec((1, D_MODEL), const),
                  pl.BlockSpec((1, D_MODEL), const),
                  pl.BlockSpec((1, D_MODEL), const),
                  pl.BlockSpec((D_MODEL, D_MODEL), const)],
        out_specs=pl.BlockSpec((1, tm, D_MODEL), tok),
        out_shape=jax.ShapeDtypeStruct(x.shape, F32),
        scratch_shapes=[pltpu.VMEM((D_MODEL // LANES, t + 2 * CONV_HALO, LANES), F32),
                        pltpu.VMEM((D_MODEL // LANES, tm, LANES), F32)],
        compiler_params=_params("arbitrary", "arbitrary", vmem=VMEM_LIMIT_BIG),
        name="conv_out_ctx" if is_ctx else "conv_out",
    )(u, g, x, mod, dw_tiles, dw_b.reshape(1, -1), ln_w.reshape(1, -1), ln_b.reshape(1, -1), w_bf)


def _rope_tables(n_tokens):
    pos = jnp.arange(n_tokens)
    pos2 = jnp.stack([pos // GRID_W, pos % GRID_W], axis=-1).astype(F32)
    inv_freq = ROPE_THETA ** (-jnp.arange(ROPE_FREQS, dtype=F32) / ROPE_FREQS)
    ang = pos2[:, :, None] * inv_freq
    cos, sin = jnp.cos(ang), jnp.sin(ang)
    cos64 = jnp.concatenate([cos[:, 0], cos[:, 0], cos[:, 1], cos[:, 1]], axis=-1)
    sin64 = jnp.concatenate([-sin[:, 0], sin[:, 0], -sin[:, 1], sin[:, 1]], axis=-1)
    return jnp.tile(cos64, (1, 2)), jnp.tile(sin64, (1, 2))


def _head_mean_matrix():
    idx = np.arange(MXU_DIM) // HEAD_DIM
    return jnp.asarray((idx[:, None] == idx[None, :]).astype(np.float32) / HEAD_DIM, dtype=BF16)


def kernel(x, c, ctx, c_ctx, l0_ada_w, l0_ada_b, l0_norm_w, l0_w_in, l0_q_norm, l0_k_norm, l0_w_out, l1_ada_w, l1_ada_b, l1_norm_w, l1_w_in, l1_w_out, l2_ada_w, l2_ada_b, l2_norm_w, l2_w_in, l2_dw_w, l2_dw_b, l2_ln_w, l2_ln_b, l2_w_out, l3_ada_w, l3_ada_b, l3_norm_w, l3_w_in, l3_q_norm, l3_k_norm, l3_w_out):
    n_batch, seq, _ = x.shape
    n_ctx = ctx.shape[1]

    rows = -(-(n_batch + 1) // SUBLANES) * SUBLANES
    cc = jnp.concatenate([c, c_ctx[None, :], jnp.zeros((rows - n_batch - 1, D_MODEL), F32)], axis=0)

    def mods(ada_w, ada_b):
        return _ada(cc, ada_w, ada_b).reshape(rows, 1, 3 * D_MODEL)

    cos, sin = _rope_tables(seq)
    cos_ctx = jnp.ones((n_ctx, LANES), F32)
    sin_ctx = jnp.zeros((n_ctx, LANES), F32)
    e_mat = _head_mean_matrix()

    def attention_layer(xl, xc, ada_w, ada_b, nw, w_in, qn, kn, w_out, need_ctx):
        mod = mods(ada_w, ada_b)
        nw2 = nw.reshape(1, -1)
        w_bf = w_in.astype(BF16)
        wo_bf = w_out.astype(BF16)
        qn_t = jnp.tile(qn, N_HEADS).reshape(1, -1)
        kn_t = jnp.tile(kn, N_KV_HEADS).reshape(1, -1)
        q, k, v, g = _attn_in(xl, mod, nw2, w_bf, e_mat, qn_t, kn_t, cos, sin, False)
        ctx_out = _attn_in(xc, mod, nw2, w_bf, e_mat, qn_t, kn_t, cos_ctx, sin_ctx, True,
                           keys_only=not need_ctx)
        kc, vc = (ctx_out[1], ctx_out[2]) if need_ctx else ctx_out
        a = _attention(q, g, [(kc, vc), (k, v)], "attention")
        xl_new = _proj_out(a, xl, mod, wo_bf, False, "attn_out")
        if need_ctx:
            ac = _attention(ctx_out[0], ctx_out[3], [(kc, vc)], "attention_ctx")
            xc = _proj_out(ac, xc, mod, wo_bf, True, "attn_out_ctx")
        return xl_new, xc

    def fourier_layer(xl, xc, ada_w, ada_b, nw, w_in, w_out):
        mod = mods(ada_w, ada_b)
        nw2 = nw.reshape(1, -1)
        w_bf = w_in.astype(BF16)
        wo_bf = w_out.astype(BF16)
        cch, sch = _dft_mats(FOURIER_GROUP_DIM, FOURIER_GROUP_DIM ** -0.5)
        cch, sch = jnp.asarray(cch, dtype=BF16), jnp.asarray(sch, dtype=BF16)
        outs = []
        for t_arr, is_ctx in ((xl, False), (xc, True)):
            t = t_arr.shape[1]
            cp, sp = _dft_mats(t, t ** -0.5)
            dft = jnp.asarray(np.concatenate([cp[:, :t // 2], -sp[:, :t // 2]], axis=1), dtype=BF16)
            rt = min(MXU_DIM, t // 2)
            rev = jnp.asarray(np.eye(rt)[::-1], dtype=BF16)
            ucs, g = _fourier_in(t_arr, mod, nw2, w_bf, cch, sch, is_ctx)
            outs.append(_fourier_out(dft, rev, ucs, g, t_arr, mod, wo_bf, is_ctx))
        return outs[0], outs[1]

    def conv_layer(xl, xc, ada_w, ada_b, nw, w_in, dw_w, dw_b, ln_w, ln_b, w_out):
        mod = mods(ada_w, ada_b)
        nw2 = nw.reshape(1, -1)
        w_bf = w_in.astype(BF16)
        wo_bf = w_out.astype(BF16)
        outs = []
        for t_arr, is_ctx in ((xl, False), (xc, True)):
            u, g = _conv_in(t_arr, mod, nw2, w_bf, is_ctx)
            outs.append(_conv_out(u, g, t_arr, mod, dw_w, dw_b, ln_w, ln_b, wo_bf, is_ctx))
        return outs[0], outs[1]

    xl, xc = x, ctx
    xl, xc = attention_layer(xl, xc, l0_ada_w, l0_ada_b, l0_norm_w, l0_w_in, l0_q_norm, l0_k_norm,
                             l0_w_out, True)
    xl, xc = fourier_layer(xl, xc, l1_ada_w, l1_ada_b, l1_norm_w, l1_w_in, l1_w_out)
    xl, xc = conv_layer(xl, xc, l2_ada_w, l2_ada_b, l2_norm_w, l2_w_in, l2_dw_w, l2_dw_b,
                        l2_ln_w, l2_ln_b, l2_w_out)
    xl, xc = attention_layer(xl, xc, l3_ada_w, l3_ada_b, l3_norm_w, l3_w_in, l3_q_norm, l3_k_norm,
                             l3_w_out, False)
    return xl
```

```python
import functools
import math

import numpy as np
import jax
import jax.numpy as jnp
from jax import lax
from jax.experimental import pallas as pl
from jax.experimental.pallas import tpu as pltpu

D_MODEL = 1024
N_HEADS = 16
N_KV_HEADS = 4
HEAD_DIM = 64
GQA_GROUP = N_HEADS // N_KV_HEADS
Q_W = N_HEADS * HEAD_DIM
KV_W = N_KV_HEADS * HEAD_DIM
GRID_W = 64
ROPE_FREQS = HEAD_DIM // 4
ROPE_THETA = 10000.0
FOURIER_GROUP_DIM = 256
CONV_WIDTH = 31
CONV_PAD = CONV_WIDTH // 2
EPS = 1e-6

LANES = 128
SUBLANES = 8
MXU_DIM = 256
VMEM_LIMIT = 48 * 1024 * 1024
VMEM_LIMIT_BIG = 56 * 1024 * 1024

F32 = jnp.float32
BF16 = jnp.bfloat16

TOKEN_TILE = 512
SEQ_TILE = 256
Q_TILE = 256
CONV_HALO = 2 * SUBLANES
CONV_BLOCK_ROWS = 64


def _params(*sem, vmem=VMEM_LIMIT):
    return pltpu.CompilerParams(dimension_semantics=sem, vmem_limit_bytes=vmem)


def _dot(a, b):
    return jnp.dot(a, b, preferred_element_type=F32)


def _silu(t):
    return t / (1.0 + jnp.exp(-t))


def _sigmoid(t):
    return 1.0 / (1.0 + jnp.exp(-t))


def _split_bf16(t):
    hi = t.astype(BF16)
    lo = (t - hi.astype(F32)).astype(BF16)
    return hi, lo


def _ada_kernel(cc_ref, w_ref, b_ref, o_ref):
    a_hi, a_lo = _split_bf16(_silu(cc_ref[...]))
    w_hi, w_lo = _split_bf16(w_ref[...])
    o_ref[...] = _dot(a_hi, w_hi) + (_dot(a_hi, w_lo) + _dot(a_lo, w_hi)) + b_ref[...]


def _ada(cc, ada_w, ada_b):
    rows = cc.shape[0]
    n = ada_w.shape[1]
    tn = D_MODEL
    return pl.pallas_call(
        _ada_kernel,
        grid=(n // tn,),
        in_specs=[pl.BlockSpec((rows, D_MODEL), lambda j: (0, 0)),
                  pl.BlockSpec((D_MODEL, tn), lambda j: (0, j)),
                  pl.BlockSpec((1, tn), lambda j: (0, j))],
        out_specs=pl.BlockSpec((rows, tn), lambda j: (0, j)),
        out_shape=jax.ShapeDtypeStruct((rows, n), F32),
        compiler_params=_params("arbitrary"),
        name="ada",
    )(cc, ada_w, ada_b.reshape(1, n))


def _modnorm(x, mod, nw):
    r = lax.rsqrt(jnp.mean(x * x, axis=-1, keepdims=True) + EPS)
    shift = mod[:, :D_MODEL]
    scale = mod[:, D_MODEL:2 * D_MODEL]
    return ((x * r) * (nw * (1.0 + scale)) + shift).astype(BF16)


def _mod_index(is_ctx, n_batch):
    if is_ctx:
        return lambda b, i: (n_batch, 0, 0)
    return lambda b, i: (b, 0, 0)


def _tile(t, rows=TOKEN_TILE):
    return min(rows, t)


def _head_rms(t, e_ref, w):
    outs = []
    for c in range(t.shape[1] // MXU_DIM):
        tc = t[:, c * MXU_DIM:(c + 1) * MXU_DIM]
        ms = _dot((tc * tc).astype(BF16), e_ref[...])
        outs.append(tc * lax.rsqrt(ms + EPS))
    return jnp.concatenate(outs, axis=1) * w


def _rope(t, cos, sin):
    lane = lax.broadcasted_iota(jnp.int32, (t.shape[0], LANES), 1)
    low_half = (lane & ROPE_FREQS) == 0
    outs = []
    for c in range(t.shape[1] // LANES):
        tc = t[:, c * LANES:(c + 1) * LANES]
        partner = jnp.where(low_half,
                            pltpu.roll(tc, LANES - ROPE_FREQS, 1),
                            pltpu.roll(tc, ROPE_FREQS, 1))
        outs.append(tc * cos + partner * sin)
    return jnp.concatenate(outs, axis=1)


def _attn_in_kernel(x_ref, mod_ref, nw_ref, w_ref, e_ref, qn_ref, kn_ref, cos_ref, sin_ref,
                    *out_refs, keys_only):
    h = _modnorm(x_ref[0], mod_ref[0], nw_ref[...])
    cos = cos_ref[...]
    sin = sin_ref[...]
    if keys_only:
        k_ref, v_ref = out_refs
    else:
        q_ref, k_ref, v_ref, g_ref = out_refs
        q = _head_rms(_dot(h, w_ref[:, :Q_W]), e_ref, qn_ref[...])
        q_ref[0] = (_rope(q, cos, sin) * (HEAD_DIM ** -0.5)).astype(BF16)
    k = _head_rms(_dot(h, w_ref[:, Q_W:Q_W + KV_W]), e_ref, kn_ref[...])
    k_ref[0] = _rope(k, cos, sin).astype(BF16)
    if not keys_only:
        g_ref[0] = _silu(_dot(h, w_ref[:, Q_W + 2 * KV_W:])).astype(BF16)
    v_ref[0] = _dot(h, w_ref[:, Q_W + KV_W:Q_W + 2 * KV_W]).astype(BF16)


def _attn_in(x, mod, nw, w_bf, e_mat, qn, kn, cos, sin, is_ctx, keys_only=False):
    b, t, _ = x.shape
    tm = _tile(t)
    n_in = w_bf.shape[1]
    tok = lambda bb, i: (bb, i, 0)
    const = lambda bb, i: (0, 0)
    widths = (KV_W, KV_W) if keys_only else (Q_W, KV_W, KV_W, Q_W)
    return pl.pallas_call(
        functools.partial(_attn_in_kernel, keys_only=keys_only),
        grid=(b, t // tm),
        in_specs=[pl.BlockSpec((1, tm, D_MODEL), tok),
                  pl.BlockSpec((1, 1, 3 * D_MODEL), _mod_index(is_ctx, b)),
                  pl.BlockSpec((1, D_MODEL), const),
                  pl.BlockSpec((D_MODEL, n_in), const),
                  pl.BlockSpec((MXU_DIM, MXU_DIM), const),
                  pl.BlockSpec((1, Q_W), const),
                  pl.BlockSpec((1, KV_W), const),
                  pl.BlockSpec((tm, LANES), lambda bb, i: (i, 0)),
                  pl.BlockSpec((tm, LANES), lambda bb, i: (i, 0))],
        out_specs=[pl.BlockSpec((1, tm, w), tok) for w in widths],
        out_shape=[jax.ShapeDtypeStruct((b, t, w), BF16) for w in widths],
        compiler_params=_params("arbitrary", "arbitrary"),
        name="attn_in_ctx" if is_ctx else "attn_in",
    )(x, mod, nw, w_bf, e_mat, qn, kn, cos, sin)


V_ROWS = HEAD_DIM + 16

def _attn_kernel(*refs, lens, n_qtiles, n_units):
    n_src = len(lens)
    q_ref, g_ref = refs[0], refs[1]
    k_refs = refs[2:2 + n_src]
    v_refs = refs[2 + n_src:2 + 2 * n_src]
    o_ref = refs[2 + 2 * n_src]
    k_lo, k_hi, v_t, s_buf, m_buf, o_buf = refs[3 + 2 * n_src:]
    n = pl.program_id(0)
    n_keys = sum(lens)
    half = LANES // 2
    units_per_batch = N_KV_HEADS * n_qtiles

    scored = jnp.minimum(n, n_units - 1)
    emitted = jnp.clip(n - 1, 0, n_units - 1)

    @pl.when(n == 0)
    def _fill():
        s_buf[1] = jnp.zeros(s_buf.shape[1:], F32)
        m_buf[1] = jnp.zeros(m_buf.shape[1:], F32)
        o_buf[1] = jnp.ones(o_buf.shape[1:], F32)

    @pl.when(scored % units_per_batch == 0)
    def _stage_k():
        zeros = jnp.zeros((n_keys, half), BF16)
        for gg in range(N_KV_HEADS):
            k_lo[gg, :, half:] = zeros
            k_hi[gg, :, :half] = zeros
            off = 0
            for s in range(n_src):
                kk = k_refs[s][0, :, gg * HEAD_DIM:(gg + 1) * HEAD_DIM]
                k_lo[gg, off:off + lens[s], :half] = kk
                k_hi[gg, off:off + lens[s], half:] = kk
                off += lens[s]

    @pl.when(emitted % units_per_batch == 0)
    def _stage_v():
        row = lax.broadcasted_iota(jnp.int32, (V_ROWS - HEAD_DIM, n_keys), 0)
        ones_row = jnp.where(row == 0, 1.0, 0.0).astype(BF16)
        for gg in range(N_KV_HEADS):
            v_t[gg, HEAD_DIM:, :] = ones_row
        off = 0
        for s in range(n_src):
            vt = v_refs[s][0].astype(F32).T.astype(BF16)
            for gg in range(N_KV_HEADS):
                v_t[gg, :HEAD_DIM, off:off + lens[s]] = vt[gg * HEAD_DIM:(gg + 1) * HEAD_DIM, :]
            off += lens[s]

    def col_max(t):
        while t.shape[0] % (2 * SUBLANES) == 0:
            m = t.shape[0] // 2
            t = jnp.maximum(t[:m, :], t[m:, :])
        return jnp.max(t, axis=0, keepdims=True)

    def step(write_slot, read_slot):
        grp = (scored // n_qtiles) % N_KV_HEADS
        vt_g = v_t[(emitted // n_qtiles) % N_KV_HEADS]
        def score(h):
            q2 = q_ref[0, :, (h // 2) * LANES:(h // 2 + 1) * LANES]
            k_st = k_hi if h % 2 else k_lo
            s_t = lax.dot_general(k_st[grp], q2, (((1,), (1,)), ((), ())),
                                  preferred_element_type=F32)
            s_buf[write_slot, h] = s_t
            m_buf[write_slot, h] = col_max(s_t)

        def weigh(h):
            p_t = jnp.exp(s_buf[read_slot, h] - m_buf[read_slot, h]).astype(BF16)
            o_buf[write_slot, h] = _dot(vt_g, p_t)

        for pair in range(GQA_GROUP // 2):
            score(2 * pair)
            weigh(2 * pair)
            weigh(2 * pair + 1)
            score(2 * pair + 1)
        o_rows = [o_buf[read_slot, h, :HEAD_DIM, :] / o_buf[read_slot, h, HEAD_DIM:HEAD_DIM + 1, :]
                  for h in range(GQA_GROUP)]
        o = jnp.concatenate(o_rows, axis=0).T
        o_ref[0] = (o * g_ref[0].astype(F32)).astype(BF16)

    pl.when(n % 2 == 0)(lambda: step(0, 1))
    pl.when(n % 2 == 1)(lambda: step(1, 0))


def _attention(q, g, kvs, name):
    b, t, _ = q.shape
    tq = min(Q_TILE, t)
    n_qtiles = t // tq
    n_units = b * N_KV_HEADS * n_qtiles
    lens = tuple(int(k.shape[1]) for k, _ in kvs)
    n_keys = sum(lens)

    def unit_block(u):
        return (u // (N_KV_HEADS * n_qtiles), u % n_qtiles, (u // n_qtiles) % N_KV_HEADS)

    def unit_batch(u):
        return (u // (N_KV_HEADS * n_qtiles), 0, 0)

    scored = lambda n: jnp.minimum(n, n_units - 1)
    emitted = lambda n: jnp.clip(n - 1, 0, n_units - 1)
    stored = lambda n: jnp.clip(n - 2, 0, n_units - 1)
    width = GQA_GROUP * HEAD_DIM
    in_specs = [pl.BlockSpec((1, tq, width), lambda n: unit_block(scored(n))),
                pl.BlockSpec((1, tq, width), lambda n: unit_block(stored(n)))]
    in_specs += [pl.BlockSpec((1, ln, KV_W), lambda n: unit_batch(scored(n))) for ln in lens]
    in_specs += [pl.BlockSpec((1, ln, KV_W), lambda n: unit_batch(emitted(n))) for ln in lens]
    args = [q, g] + [k for k, _ in kvs] + [v for _, v in kvs]
    return pl.pallas_call(
        functools.partial(_attn_kernel, lens=lens, n_qtiles=n_qtiles, n_units=n_units),
        grid=(n_units + 2,),
        in_specs=in_specs,
        out_specs=pl.BlockSpec((1, tq, width), lambda n: unit_block(stored(n))),
        out_shape=jax.ShapeDtypeStruct((b, t, Q_W), BF16),
        scratch_shapes=[pltpu.VMEM((N_KV_HEADS, n_keys, LANES), BF16),
                        pltpu.VMEM((N_KV_HEADS, n_keys, LANES), BF16),
                        pltpu.VMEM((N_KV_HEADS, V_ROWS, n_keys), BF16),
                        pltpu.VMEM((2, GQA_GROUP, n_keys, tq), F32),
                        pltpu.VMEM((2, GQA_GROUP, 1, tq), F32),
                        pltpu.VMEM((2, GQA_GROUP, V_ROWS, tq), F32)],
        compiler_params=_params("arbitrary", vmem=VMEM_LIMIT_BIG),
        name=name,
    )(*args)


def _proj_out_kernel(a_ref, x_ref, mod_ref, w_ref, o_ref):
    gate = mod_ref[0][:, 2 * D_MODEL:]
    o_ref[0] = x_ref[0] + gate * _dot(a_ref[0], w_ref[...])


def _proj_out(a, x, mod, w_bf, is_ctx, name):
    b, t, _ = x.shape
    tm = _tile(t)
    tok = lambda bb, i: (bb, i, 0)
    return pl.pallas_call(
        _proj_out_kernel,
        grid=(b, t // tm),
        in_specs=[pl.BlockSpec((1, tm, D_MODEL), tok),
                  pl.BlockSpec((1, tm, D_MODEL), tok),
                  pl.BlockSpec((1, 1, 3 * D_MODEL), _mod_index(is_ctx, b)),
                  pl.BlockSpec((D_MODEL, D_MODEL), lambda bb, i: (0, 0))],
        out_specs=pl.BlockSpec((1, tm, D_MODEL), tok),
        out_shape=jax.ShapeDtypeStruct(x.shape, F32),
        compiler_params=_params("arbitrary", "arbitrary"),
        name=name,
    )(a, x, mod, w_bf)


def _dft_mats(n, scale):
    idx = np.arange(n, dtype=np.int64)
    ang = 2.0 * np.pi * ((idx[:, None] * idx[None, :]) % n).astype(np.float64) / n
    return np.cos(ang) * scale, np.sin(ang) * scale


def _fourier_in_kernel(x_ref, mod_ref, nw_ref, w_ref, cc_ref, sc_ref, uc_ref, g_ref):
    h = _modnorm(x_ref[0], mod_ref[0], nw_ref[...])
    u = _dot(h, w_ref[:, :D_MODEL]).astype(BF16)
    ucs, uss = [], []
    for c in range(D_MODEL // FOURIER_GROUP_DIM):
        ug = u[:, c * FOURIER_GROUP_DIM:(c + 1) * FOURIER_GROUP_DIM]
        ucs.append(_dot(ug, cc_ref[...]))
        uss.append(_dot(ug, sc_ref[...]))
    uc_ref[0, 0] = jnp.concatenate(ucs, axis=1).astype(BF16)
    uc_ref[0, 1] = jnp.concatenate(uss, axis=1).astype(BF16)
    g_ref[0] = _silu(_dot(h, w_ref[:, D_MODEL:])).astype(BF16)


def _fourier_in(x, mod, nw, w_bf, cmat, smat, is_ctx):
    b, t, _ = x.shape
    tm = _tile(t)
    tok = lambda bb, i: (bb, i, 0)
    const = lambda bb, i: (0, 0)
    return pl.pallas_call(
        _fourier_in_kernel,
        grid=(b, t // tm),
        in_specs=[pl.BlockSpec((1, tm, D_MODEL), tok),
                  pl.BlockSpec((1, 1, 3 * D_MODEL), _mod_index(is_ctx, b)),
                  pl.BlockSpec((1, D_MODEL), const),
                  pl.BlockSpec((D_MODEL, 2 * D_MODEL), const),
                  pl.BlockSpec((FOURIER_GROUP_DIM, FOURIER_GROUP_DIM), const),
                  pl.BlockSpec((FOURIER_GROUP_DIM, FOURIER_GROUP_DIM), const)],
        out_specs=[pl.BlockSpec((1, 2, tm, D_MODEL), lambda bb, i: (bb, 0, i, 0)),
                   pl.BlockSpec((1, tm, D_MODEL), tok)],
        out_shape=[jax.ShapeDtypeStruct((b, 2, t, D_MODEL), BF16),
                   jax.ShapeDtypeStruct((b, t, D_MODEL), BF16)],
        compiler_params=_params("arbitrary", "arbitrary"),
        name="fourier_in_ctx" if is_ctx else "fourier_in",
    )(x, mod, nw, w_bf, cmat, smat)


def _fourier_out_kernel(dft_ref, rev_ref, uc_ref, g_ref, x_ref, mod_ref, w_ref, o_ref, folded):
    n_pos = uc_ref.shape[1] // 2
    half = n_pos // 2
    tm = o_ref.shape[1]
    rt = rev_ref.shape[0]

    @pl.when(pl.program_id(1) == 0)
    def _fold():
        first_row = lax.broadcasted_iota(jnp.int32, (half, D_MODEL), 0) == 0
        for part, sign in ((0, 1.0), (1, -1.0)):
            base = part * n_pos
            tiles = [_dot(rev_ref[...], uc_ref[0, pl.ds(base + n_pos - (j + 1) * rt, rt), :])
                     for j in range(half // rt)]
            rev = jnp.concatenate(tiles, axis=0)
            mirror = jnp.where(first_row, 0.0, pltpu.roll(rev, 1, 0))
            folded[pl.ds(part * half, half), :] = (
                uc_ref[0, pl.ds(base, half), :].astype(F32) + sign * mirror).astype(BF16)

    f = _dot(dft_ref[...], folded[...])
    mid = uc_ref[0, half:half + 2 * SUBLANES, :].astype(F32)[0:1, :] * (n_pos ** -0.5)
    odd = (lax.broadcasted_iota(jnp.int32, (tm, D_MODEL), 0) & 1) == 1
    f = f + jnp.where(odd, -mid, mid)
    a = (f * g_ref[0].astype(F32)).astype(BF16)
    gate = mod_ref[0][:, 2 * D_MODEL:]
    o_ref[0] = x_ref[0] + gate * _dot(a, w_ref[...])


def _fourier_out(dft, rev, ucs, g, x, mod, w_bf, is_ctx):
    b, t, _ = x.shape
    tm = _tile(t, SEQ_TILE)
    tok = lambda bb, i: (bb, i, 0)
    return pl.pallas_call(
        _fourier_out_kernel,
        grid=(b, t // tm),
        in_specs=[pl.BlockSpec((tm, t), lambda bb, i: (i, 0)),
                  pl.BlockSpec(rev.shape, lambda bb, i: (0, 0)),
                  pl.BlockSpec((1, 2 * t, D_MODEL), lambda bb, i: (bb, 0, 0)),
                  pl.BlockSpec((1, tm, D_MODEL), tok),
                  pl.BlockSpec((1, tm, D_MODEL), tok),
                  pl.BlockSpec((1, 1, 3 * D_MODEL), _mod_index(is_ctx, b)),
                  pl.BlockSpec((D_MODEL, D_MODEL), lambda bb, i: (0, 0))],
        out_specs=pl.BlockSpec((1, tm, D_MODEL), tok),
        out_shape=jax.ShapeDtypeStruct(x.shape, F32),
        scratch_shapes=[pltpu.VMEM((t, D_MODEL), BF16)],
        compiler_params=_params("arbitrary", "arbitrary", vmem=VMEM_LIMIT_BIG),
        name="fourier_out_ctx" if is_ctx else "fourier_out",
    )(dft, rev, ucs.reshape(b, 2 * t, D_MODEL), g, x, mod, w_bf)


def _conv_in_kernel(x_ref, mod_ref, nw_ref, w_ref, u_ref, g_ref):
    h = _modnorm(x_ref[0], mod_ref[0], nw_ref[...])
    a = _dot(h, w_ref[:, :D_MODEL])
    gl = _dot(h, w_ref[:, D_MODEL:2 * D_MODEL])
    u_ref[0] = a * _sigmoid(gl)
    g_ref[0] = _silu(_dot(h, w_ref[:, 2 * D_MODEL:])).astype(BF16)


def _conv_in(x, mod, nw, w_bf, is_ctx):
    b, t, _ = x.shape
    tm = _tile(t)
    tok = lambda bb, i: (bb, i, 0)
    const = lambda bb, i: (0, 0)
    return pl.pallas_call(
        _conv_in_kernel,
        grid=(b, t // tm),
        in_specs=[pl.BlockSpec((1, tm, D_MODEL), tok),
                  pl.BlockSpec((1, 1, 3 * D_MODEL), _mod_index(is_ctx, b)),
                  pl.BlockSpec((1, D_MODEL), const),
                  pl.BlockSpec((D_MODEL, 3 * D_MODEL), const)],
        out_specs=[pl.BlockSpec((1, tm, D_MODEL), tok),
                   pl.BlockSpec((1, tm, D_MODEL), tok)],
        out_shape=[jax.ShapeDtypeStruct((b, t, D_MODEL), F32),
                   jax.ShapeDtypeStruct((b, t, D_MODEL), BF16)],
        compiler_params=_params("arbitrary", "arbitrary"),
        name="conv_in_ctx" if is_ctx else "conv_in",
    )(x, mod, nw, w_bf)


def _conv_out_kernel(u_ref, g_ref, x_ref, mod_ref, dw_ref, dwb_ref, lnw_ref, lnb_ref, w_ref,
                     o_ref, upad, conv):
    t = u_ref.shape[1]
    tm = o_ref.shape[1]
    halo = CONV_HALO
    n_cols = D_MODEL // LANES
    i = pl.program_id(1)

    @pl.when(i == 0)
    def _stage():
        for col in range(n_cols):
            upad[col, 0:halo, :] = jnp.zeros((halo, LANES), F32)
            upad[col, halo:halo + t, :] = u_ref[0, :, col * LANES:(col + 1) * LANES]
            upad[col, halo + t:, :] = jnp.zeros((halo, LANES), F32)

    groups = CONV_BLOCK_ROWS // SUBLANES
    row0 = pl.multiple_of(i * tm, SUBLANES)

    def block(n, carry):
        col = n % n_cols
        base = pl.multiple_of((n // n_cols) * CONV_BLOCK_ROWS, CONV_BLOCK_ROWS)
        acc = jnp.zeros((groups, SUBLANES, LANES), F32)
        for k in range(CONV_WIDTH):
            rows = upad[col, pl.ds(row0 + base + (halo - CONV_PAD + k), CONV_BLOCK_ROWS), :]
            acc = acc + rows.reshape(groups, SUBLANES, LANES) * dw_ref[k, col][None]
        conv[col, pl.ds(base, CONV_BLOCK_ROWS), :] = acc.reshape(CONV_BLOCK_ROWS, LANES)
        return carry

    lax.fori_loop(0, (tm // CONV_BLOCK_ROWS) * n_cols, block, 0, unroll=2)

    c = jnp.concatenate([conv[col] for col in range(n_cols)], axis=1) + dwb_ref[...]
    mu = jnp.mean(c, axis=-1, keepdims=True)
    cen = c - mu
    var = jnp.mean(cen * cen, axis=-1, keepdims=True)
    ln = cen * lax.rsqrt(var + EPS) * lnw_ref[...] + lnb_ref[...]
    a = (_silu(ln) * g_ref[0].astype(F32)).astype(BF16)
    gate = mod_ref[0][:, 2 * D_MODEL:]
    o_ref[0] = x_ref[0] + gate * _dot(a, w_ref[...])


def _conv_out(u, g, x, mod, dw_w, dw_b, ln_w, ln_b, w_bf, is_ctx):
    b, t, _ = x.shape
    tm = _tile(t, SEQ_TILE)
    dw_tiles = jnp.broadcast_to(dw_w.reshape(CONV_WIDTH, D_MODEL // LANES, 1, LANES),
                                (CONV_WIDTH, D_MODEL // LANES, SUBLANES, LANES))
    tok = lambda bb, i: (bb, i, 0)
    const = lambda bb, i: (0, 0)
    return pl.pallas_call(
        _conv_out_kernel,
        grid=(b, t // tm),
        in_specs=[pl.BlockSpec((1, t, D_MODEL), lambda bb, i: (bb, 0, 0)),
                  pl.BlockSpec((1, tm, D_MODEL), tok),
                  pl.BlockSpec((1, tm, D_MODEL), tok),
                  pl.BlockSpec((1, 1, 3 * D_MODEL), _mod_index(is_ctx, b)),
                  pl.BlockSpec((CONV_WIDTH, D_MODEL // LANES, SUBLANES, LANES),
                               lambda bb, i: (0, 0, 0, 0)),
                  pl.BlockSpec((1, D_MODEL), const),
                  pl.BlockSpec((1, D_MODEL), const),
                  pl.BlockSpec((1, D_MODEL), const),
                  pl.BlockSpec((D_MODEL, D_MODEL), const)],
        out_specs=pl.BlockSpec((1, tm, D_MODEL), tok),
        out_shape=jax.ShapeDtypeStruct(x.shape, F32),
        scratch_shapes=[pltpu.VMEM((D_MODEL // LANES, t + 2 * CONV_HALO, LANES), F32),
                        pltpu.VMEM((D_MODEL // LANES, tm, LANES), F32)],
        compiler_params=_params("arbitrary", "arbitrary", vmem=VMEM_LIMIT_BIG),
        name="conv_out_ctx" if is_ctx else "conv_out",
    )(u, g, x, mod, dw_tiles, dw_b.reshape(1, -1), ln_w.reshape(1, -1), ln_b.reshape(1, -1), w_bf)


def _rope_tables(n_tokens):
    pos = jnp.arange(n_tokens)
    pos2 = jnp.stack([pos // GRID_W, pos % GRID_W], axis=-1).astype(F32)
    inv_freq = ROPE_THETA ** (-jnp.arange(ROPE_FREQS, dtype=F32) / ROPE_FREQS)
    ang = pos2[:, :, None] * inv_freq
    cos, sin = jnp.cos(ang), jnp.sin(ang)
    cos64 = jnp.concatenate([cos[:, 0], cos[:, 0], cos[:, 1], cos[:, 1]], axis=-1)
    sin64 = jnp.concatenate([-sin[:, 0], sin[:, 0], -sin[:, 1], sin[:, 1]], axis=-1)
    return jnp.tile(cos64, (1, 2)), jnp.tile(sin64, (1, 2))


def _head_mean_matrix():
    idx = np.arange(MXU_DIM) // HEAD_DIM
    return jnp.asarray((idx[:, None] == idx[None, :]).astype(np.float32) / HEAD_DIM, dtype=BF16)


def kernel(x, c, ctx, c_ctx, l0_ada_w, l0_ada_b, l0_norm_w, l0_w_in, l0_q_norm, l0_k_norm, l0_w_out, l1_ada_w, l1_ada_b, l1_norm_w, l1_w_in, l1_w_out, l2_ada_w, l2_ada_b, l2_norm_w, l2_w_in, l2_dw_w, l2_dw_b, l2_ln_w, l2_ln_b, l2_w_out, l3_ada_w, l3_ada_b, l3_norm_w, l3_w_in, l3_q_norm, l3_k_norm, l3_w_out):
    n_batch, seq, _ = x.shape
    n_ctx = ctx.shape[1]

    rows = -(-(n_batch + 1) // SUBLANES) * SUBLANES
    cc = jnp.concatenate([c, c_ctx[None, :], jnp.zeros((rows - n_batch - 1, D_MODEL), F32)], axis=0)

    def mods(ada_w, ada_b):
        return _ada(cc, ada_w, ada_b).reshape(rows, 1, 3 * D_MODEL)

    cos, sin = _rope_tables(seq)
    cos_ctx = jnp.ones((n_ctx, LANES), F32)
    sin_ctx = jnp.zeros((n_ctx, LANES), F32)
    e_mat = _head_mean_matrix()

    def attention_layer(xl, xc, ada_w, ada_b, nw, w_in, qn, kn, w_out, need_ctx):
        mod = mods(ada_w, ada_b)
        nw2 = nw.reshape(1, -1)
        w_bf = w_in.astype(BF16)
        wo_bf = w_out.astype(BF16)
        qn_t = jnp.tile(qn, N_HEADS).reshape(1, -1)
        kn_t = jnp.tile(kn, N_KV_HEADS).reshape(1, -1)
        q, k, v, g = _attn_in(xl, mod, nw2, w_bf, e_mat, qn_t, kn_t, cos, sin, False)
        ctx_out = _attn_in(xc, mod, nw2, w_bf, e_mat, qn_t, kn_t, cos_ctx, sin_ctx, True,
                           keys_only=not need_ctx)
        kc, vc = (ctx_out[1], ctx_out[2]) if need_ctx else ctx_out
        a = _attention(q, g, [(kc, vc), (k, v)], "attention")
        xl_new = _proj_out(a, xl, mod, wo_bf, False, "attn_out")
        if need_ctx:
            ac = _attention(ctx_out[0], ctx_out[3], [(kc, vc)], "attention_ctx")
            xc = _proj_out(ac, xc, mod, wo_bf, True, "attn_out_ctx")
        return xl_new, xc

    def fourier_layer(xl, xc, ada_w, ada_b, nw, w_in, w_out):
        mod = mods(ada_w, ada_b)
        nw2 = nw.reshape(1, -1)
        w_bf = w_in.astype(BF16)
        wo_bf = w_out.astype(BF16)
        cch, sch = _dft_mats(FOURIER_GROUP_DIM, FOURIER_GROUP_DIM ** -0.5)
        cch, sch = jnp.asarray(cch, dtype=BF16), jnp.asarray(sch, dtype=BF16)
        outs = []
        for t_arr, is_ctx in ((xl, False), (xc, True)):
            t = t_arr.shape[1]
            cp, sp = _dft_mats(t, t ** -0.5)
            dft = jnp.asarray(np.concatenate([cp[:, :t // 2], -sp[:, :t // 2]], axis=1), dtype=BF16)
            rt = min(MXU_DIM, t // 2)
            rev = jnp.asarray(np.eye(rt)[::-1], dtype=BF16)
            ucs, g = _fourier_in(t_arr, mod, nw2, w_bf, cch, sch, is_ctx)
            outs.append(_fourier_out(dft, rev, ucs, g, t_arr, mod, wo_bf, is_ctx))
        return outs[0], outs[1]

    def conv_layer(xl, xc, ada_w, ada_b, nw, w_in, dw_w, dw_b, ln_w, ln_b, w_out):
        mod = mods(ada_w, ada_b)
        nw2 = nw.reshape(1, -1)
        w_bf = w_in.astype(BF16)
        wo_bf = w_out.astype(BF16)
        outs = []
        for t_arr, is_ctx in ((xl, False), (xc, True)):
            u, g = _conv_in(t_arr, mod, nw2, w_bf, is_ctx)
            outs.append(_conv_out(u, g, t_arr, mod, dw_w, dw_b, ln_w, ln_b, wo_bf, is_ctx))
        return outs[0], outs[1]

    xl, xc = x, ctx
    xl, xc = attention_layer(xl, xc, l0_ada_w, l0_ada_b, l0_norm_w, l0_w_in, l0_q_norm, l0_k_norm,
                             l0_w_out, True)
    xl, xc = fourier_layer(xl, xc, l1_ada_w, l1_ada_b, l1_norm_w, l1_w_in, l1_w_out)
    xl, xc = conv_layer(xl, xc, l2_ada_w, l2_ada_b, l2_norm_w, l2_w_in, l2_dw_w, l2_dw_b,
                        l2_ln_w, l2_ln_b, l2_w_out)
    xl, xc = attention_layer(xl, xc, l3_ada_w, l3_ada_b, l3_norm_w, l3_w_in, l3_q_norm, l3_k_norm,
                             l3_w_out, False)
    return xl
```

```python
import functools
import math

import numpy as np
import jax
import jax.numpy as jnp
from jax import lax
from jax.experimental import pallas as pl
from jax.experimental.pallas import tpu as pltpu

D_MODEL = 1024
N_HEADS = 16
N_KV_HEADS = 4
HEAD_DIM = 64
GQA_GROUP = N_HEADS // N_KV_HEADS
Q_W = N_HEADS * HEAD_DIM
KV_W = N_KV_HEADS * HEAD_DIM
GRID_W = 64
ROPE_FREQS = HEAD_DIM // 4
ROPE_THETA = 10000.0
FOURIER_GROUP_DIM = 256
CONV_WIDTH = 31
CONV_PAD = CONV_WIDTH // 2
EPS = 1e-6

LANES = 128
SUBLANES = 8
MXU_DIM = 256
VMEM_LIMIT = 48 * 1024 * 1024
VMEM_LIMIT_BIG = 56 * 1024 * 1024

F32 = jnp.float32
BF16 = jnp.bfloat16

TOKEN_TILE = 1024
SEQ_TILE = 256
Q_TILE = 256
CONV_HALO = 2 * SUBLANES
CONV_BLOCK_ROWS = 64


def _params(*sem, vmem=VMEM_LIMIT):
    return pltpu.CompilerParams(dimension_semantics=sem, vmem_limit_bytes=vmem)


def _dot(a, b):
    return jnp.dot(a, b, preferred_element_type=F32)


def _silu(t):
    return t / (1.0 + jnp.exp(-t))


def _sigmoid(t):
    return 1.0 / (1.0 + jnp.exp(-t))


def _split_bf16(t):
    hi = t.astype(BF16)
    lo = (t - hi.astype(F32)).astype(BF16)
    return hi, lo


def _ada_kernel(cc_ref, w_ref, b_ref, o_ref):
    a_hi, a_lo = _split_bf16(_silu(cc_ref[...]))
    w_hi, w_lo = _split_bf16(w_ref[...])
    o_ref[...] = _dot(a_hi, w_hi) + (_dot(a_hi, w_lo) + _dot(a_lo, w_hi)) + b_ref[...]


def _ada(cc, ada_w, ada_b):
    rows = cc.shape[0]
    n = ada_w.shape[1]
    tn = D_MODEL
    return pl.pallas_call(
        _ada_kernel,
        grid=(n // tn,),
        in_specs=[pl.BlockSpec((rows, D_MODEL), lambda j: (0, 0)),
                  pl.BlockSpec((D_MODEL, tn), lambda j: (0, j)),
                  pl.BlockSpec((1, tn), lambda j: (0, j))],
        out_specs=pl.BlockSpec((rows, tn), lambda j: (0, j)),
        out_shape=jax.ShapeDtypeStruct((rows, n), F32),
        compiler_params=_params("arbitrary"),
        name="ada",
    )(cc, ada_w, ada_b.reshape(1, n))


def _modnorm(x, mod, nw):
    r = lax.rsqrt(jnp.mean(x * x, axis=-1, keepdims=True) + EPS)
    shift = mod[:, :D_MODEL]
    scale = mod[:, D_MODEL:2 * D_MODEL]
    return ((x * r) * (nw * (1.0 + scale)) + shift).astype(BF16)


def _mod_index(is_ctx, n_batch):
    if is_ctx:
        return lambda b, i: (n_batch, 0, 0)
    return lambda b, i: (b, 0, 0)


def _tile(t, rows=TOKEN_TILE):
    return min(rows, t)


def _head_rms(t, e_ref, w):
    outs = []
    for c in range(t.shape[1] // MXU_DIM):
        tc = t[:, c * MXU_DIM:(c + 1) * MXU_DIM]
        ms = _dot((tc * tc).astype(BF16), e_ref[...])
        outs.append(tc * lax.rsqrt(ms + EPS))
    return jnp.concatenate(outs, axis=1) * w


def _rope(t, cos, sin):
    lane = lax.broadcasted_iota(jnp.int32, (t.shape[0], LANES), 1)
    low_half = (lane & ROPE_FREQS) == 0
    outs = []
    for c in range(t.shape[1] // LANES):
        tc = t[:, c * LANES:(c + 1) * LANES]
        partner = jnp.where(low_half,
                            pltpu.roll(tc, LANES - ROPE_FREQS, 1),
                            pltpu.roll(tc, ROPE_FREQS, 1))
        outs.append(tc * cos + partner * sin)
    return jnp.concatenate(outs, axis=1)


def _attn_in_kernel(x_ref, mod_ref, nw_ref, w_ref, e_ref, qn_ref, kn_ref, cos_ref, sin_ref,
                    *out_refs, keys_only):
    h = _modnorm(x_ref[0], mod_ref[0], nw_ref[...])
    cos = cos_ref[...]
    sin = sin_ref[...]
    if keys_only:
        k_ref, v_ref = out_refs
    else:
        q_ref, k_ref, v_ref, g_ref = out_refs
        q = _head_rms(_dot(h, w_ref[:, :Q_W]), e_ref, qn_ref[...])
        q_ref[0] = (_rope(q, cos, sin) * (HEAD_DIM ** -0.5)).astype(BF16)
    k = _head_rms(_dot(h, w_ref[:, Q_W:Q_W + KV_W]), e_ref, kn_ref[...])
    k_ref[0] = _rope(k, cos, sin).astype(BF16)
    if not keys_only:
        g_ref[0] = _silu(_dot(h, w_ref[:, Q_W + 2 * KV_W:])).astype(BF16)
    v_ref[0] = _dot(h, w_ref[:, Q_W + KV_W:Q_W + 2 * KV_W]).astype(BF16)


def _attn_in(x, mod, nw, w_bf, e_mat, qn, kn, cos, sin, is_ctx, keys_only=False):
    b, t, _ = x.shape
    tm = _tile(t)
    n_in = w_bf.shape[1]
    tok = lambda bb, i: (bb, i, 0)
    const = lambda bb, i: (0, 0)
    widths = (KV_W, KV_W) if keys_only else (Q_W, KV_W, KV_W, Q_W)
    return pl.pallas_call(
        functools.partial(_attn_in_kernel, keys_only=keys_only),
        grid=(b, t // tm),
        in_specs=[pl.BlockSpec((1, tm, D_MODEL), tok),
                  pl.BlockSpec((1, 1, 3 * D_MODEL), _mod_index(is_ctx, b)),
                  pl.BlockSpec((1, D_MODEL), const),
                  pl.BlockSpec((D_MODEL, n_in), const),
                  pl.BlockSpec((MXU_DIM, MXU_DIM), const),
                  pl.BlockSpec((1, Q_W), const),
                  pl.BlockSpec((1, KV_W), const),
                  pl.BlockSpec((tm, LANES), lambda bb, i: (i, 0)),
                  pl.BlockSpec((tm, LANES), lambda bb, i: (i, 0))],
        out_specs=[pl.BlockSpec((1, tm, w), tok) for w in widths],
        out_shape=[jax.ShapeDtypeStruct((b, t, w), BF16) for w in widths],
        compiler_params=_params("arbitrary", "arbitrary"),
        name="attn_in_ctx" if is_ctx else "attn_in",
    )(x, mod, nw, w_bf, e_mat, qn, kn, cos, sin)


V_ROWS = HEAD_DIM + 16

def _attn_kernel(*refs, lens, n_qtiles, n_units):
    n_src = len(lens)
    q_ref, g_ref = refs[0], refs[1]
    k_refs = refs[2:2 + n_src]
    v_refs = refs[2 + n_src:2 + 2 * n_src]
    o_ref = refs[2 + 2 * n_src]
    k_lo, k_hi, v_t, s_buf, m_buf, o_buf = refs[3 + 2 * n_src:]
    n = pl.program_id(0)
    n_keys = sum(lens)
    half = LANES // 2
    units_per_batch = N_KV_HEADS * n_qtiles

    scored = jnp.minimum(n, n_units - 1)
    emitted = jnp.clip(n - 1, 0, n_units - 1)

    @pl.when(n == 0)
    def _fill():
        s_buf[1] = jnp.zeros(s_buf.shape[1:], F32)
        m_buf[1] = jnp.zeros(m_buf.shape[1:], F32)
        o_buf[1] = jnp.ones(o_buf.shape[1:], F32)

    @pl.when(scored % units_per_batch == 0)
    def _stage_k():
        zeros = jnp.zeros((n_keys, half), BF16)
        for gg in range(N_KV_HEADS):
            k_lo[gg, :, half:] = zeros
            k_hi[gg, :, :half] = zeros
            off = 0
            for s in range(n_src):
                kk = k_refs[s][0, :, gg * HEAD_DIM:(gg + 1) * HEAD_DIM]
                k_lo[gg, off:off + lens[s], :half] = kk
                k_hi[gg, off:off + lens[s], half:] = kk
                off += lens[s]

    @pl.when(emitted % units_per_batch == 0)
    def _stage_v():
        row = lax.broadcasted_iota(jnp.int32, (V_ROWS - HEAD_DIM, n_keys), 0)
        ones_row = jnp.where(row == 0, 1.0, 0.0).astype(BF16)
        for gg in range(N_KV_HEADS):
            v_t[gg, HEAD_DIM:, :] = ones_row
        off = 0
        for s in range(n_src):
            vt = v_refs[s][0].astype(F32).T.astype(BF16)
            for gg in range(N_KV_HEADS):
                v_t[gg, :HEAD_DIM, off:off + lens[s]] = vt[gg * HEAD_DIM:(gg + 1) * HEAD_DIM, :]
            off += lens[s]

    def col_max(t):
        while t.shape[0] % (2 * SUBLANES) == 0:
            m = t.shape[0] // 2
            t = jnp.maximum(t[:m, :], t[m:, :])
        return jnp.max(t, axis=0, keepdims=True)

    def step(write_slot, read_slot):
        grp = (scored // n_qtiles) % N_KV_HEADS
        vt_g = v_t[(emitted // n_qtiles) % N_KV_HEADS]
        def score(h):
            q2 = q_ref[0, :, (h // 2) * LANES:(h // 2 + 1) * LANES]
            k_st = k_hi if h % 2 else k_lo
            s_t = lax.dot_general(k_st[grp], q2, (((1,), (1,)), ((), ())),
                                  preferred_element_type=F32)
            s_buf[write_slot, h] = s_t
            m_buf[write_slot, h] = col_max(s_t)

        def weigh(h):
            p_t = jnp.exp(s_buf[read_slot, h] - m_buf[read_slot, h]).astype(BF16)
            o_buf[write_slot, h] = _dot(vt_g, p_t)

        for pair in range(GQA_GROUP // 2):
            score(2 * pair)
            weigh(2 * pair)
            weigh(2 * pair + 1)
            score(2 * pair + 1)
        o_rows = [o_buf[read_slot, h, :HEAD_DIM, :] / o_buf[read_slot, h, HEAD_DIM:HEAD_DIM + 1, :]
                  for h in range(GQA_GROUP)]
        o = jnp.concatenate(o_rows, axis=0).T
        o_ref[0] = (o * g_ref[0].astype(F32)).astype(BF16)

    pl.when(n % 2 == 0)(lambda: step(0, 1))
    pl.when(n % 2 == 1)(lambda: step(1, 0))


def _attention(q, g, kvs, name):
    b, t, _ = q.shape
    tq = min(Q_TILE, t)
    n_qtiles = t // tq
    n_units = b * N_KV_HEADS * n_qtiles
    lens = tuple(int(k.shape[1]) for k, _ in kvs)
    n_keys = sum(lens)

    def unit_block(u):
        return (u // (N_KV_HEADS * n_qtiles), u % n_qtiles, (u // n_qtiles) % N_KV_HEADS)

    def unit_batch(u):
        return (u // (N_KV_HEADS * n_qtiles), 0, 0)

    scored = lambda n: jnp.minimum(n, n_units - 1)
    emitted = lambda n: jnp.clip(n - 1, 0, n_units - 1)
    stored = lambda n: jnp.clip(n - 2, 0, n_units - 1)
    width = GQA_GROUP * HEAD_DIM
    in_specs = [pl.BlockSpec((1, tq, width), lambda n: unit_block(scored(n))),
                pl.BlockSpec((1, tq, width), lambda n: unit_block(stored(n)))]
    in_specs += [pl.BlockSpec((1, ln, KV_W), lambda n: unit_batch(scored(n))) for ln in lens]
    in_specs += [pl.BlockSpec((1, ln, KV_W), lambda n: unit_batch(emitted(n))) for ln in lens]
    args = [q, g] + [k for k, _ in kvs] + [v for _, v in kvs]
    return pl.pallas_call(
        functools.partial(_attn_kernel, lens=lens, n_qtiles=n_qtiles, n_units=n_units),
        grid=(n_units + 2,),
        in_specs=in_specs,
        out_specs=pl.BlockSpec((1, tq, width), lambda n: unit_block(stored(n))),
        out_shape=jax.ShapeDtypeStruct((b, t, Q_W), BF16),
        scratch_shapes=[pltpu.VMEM((N_KV_HEADS, n_keys, LANES), BF16),
                        pltpu.VMEM((N_KV_HEADS, n_keys, LANES), BF16),
                        pltpu.VMEM((N_KV_HEADS, V_ROWS, n_keys), BF16),
                        pltpu.VMEM((2, GQA_GROUP, n_keys, tq), F32),
                        pltpu.VMEM((2, GQA_GROUP, 1, tq), F32),
                        pltpu.VMEM((2, GQA_GROUP, V_ROWS, tq), F32)],
        compiler_params=_params("arbitrary", vmem=VMEM_LIMIT_BIG),
        name=name,
    )(*args)


def _proj_out_kernel(a_ref, x_ref, mod_ref, w_ref, o_ref):
    gate = mod_ref[0][:, 2 * D_MODEL:]
    o_ref[0] = x_ref[0] + gate * _dot(a_ref[0], w_ref[...])


def _proj_out(a, x, mod, w_bf, is_ctx, name):
    b, t, _ = x.shape
    tm = _tile(t)
    tok = lambda bb, i: (bb, i, 0)
    return pl.pallas_call(
        _proj_out_kernel,
        grid=(b, t // tm),
        in_specs=[pl.BlockSpec((1, tm, D_MODEL), tok),
                  pl.BlockSpec((1, tm, D_MODEL), tok),
                  pl.BlockSpec((1, 1, 3 * D_MODEL), _mod_index(is_ctx, b)),
                  pl.BlockSpec((D_MODEL, D_MODEL), lambda bb, i: (0, 0))],
        out_specs=pl.BlockSpec((1, tm, D_MODEL), tok),
        out_shape=jax.ShapeDtypeStruct(x.shape, F32),
        compiler_params=_params("arbitrary", "arbitrary"),
        name=name,
    )(a, x, mod, w_bf)


def _dft_mats(n, scale):
    idx = np.arange(n, dtype=np.int64)
    ang = 2.0 * np.pi * ((idx[:, None] * idx[None, :]) % n).astype(np.float64) / n
    return np.cos(ang) * scale, np.sin(ang) * scale


def _fourier_in_kernel(*refs, after_attention):
    if after_attention:
        a_ref, pmod_ref, pw_ref = refs[:3]
        x_ref, mod_ref, nw_ref, w_ref, cc_ref, sc_ref, xo_ref, uc_ref, g_ref = refs[3:]
        x = x_ref[0] + pmod_ref[0][:, 2 * D_MODEL:] * _dot(a_ref[0], pw_ref[...])
        xo_ref[0] = x
    else:
        x_ref, mod_ref, nw_ref, w_ref, cc_ref, sc_ref, uc_ref, g_ref = refs
        x = x_ref[0]
    h = _modnorm(x, mod_ref[0], nw_ref[...])
    u = _dot(h, w_ref[:, :D_MODEL]).astype(BF16)
    ucs, uss = [], []
    for c in range(D_MODEL // FOURIER_GROUP_DIM):
        ug = u[:, c * FOURIER_GROUP_DIM:(c + 1) * FOURIER_GROUP_DIM]
        ucs.append(_dot(ug, cc_ref[...]))
        uss.append(_dot(ug, sc_ref[...]))
    uc_ref[0, 0] = jnp.concatenate(ucs, axis=1).astype(BF16)
    uc_ref[0, 1] = jnp.concatenate(uss, axis=1).astype(BF16)
    g_ref[0] = _silu(_dot(h, w_ref[:, D_MODEL:])).astype(BF16)


def _fourier_in(x, mod, nw, w_bf, cmat, smat, is_ctx, prev=None):
    b, t, _ = x.shape
    tm = _tile(t)
    tok = lambda bb, i: (bb, i, 0)
    const = lambda bb, i: (0, 0)
    mod_spec = pl.BlockSpec((1, 1, 3 * D_MODEL), _mod_index(is_ctx, b))
    in_specs = [pl.BlockSpec((1, tm, D_MODEL), tok),
                mod_spec,
                pl.BlockSpec((1, D_MODEL), const),
                pl.BlockSpec((D_MODEL, 2 * D_MODEL), const),
                pl.BlockSpec((FOURIER_GROUP_DIM, FOURIER_GROUP_DIM), const),
                pl.BlockSpec((FOURIER_GROUP_DIM, FOURIER_GROUP_DIM), const)]
    out_specs = [pl.BlockSpec((1, 2, tm, D_MODEL), lambda bb, i: (bb, 0, i, 0)),
                 pl.BlockSpec((1, tm, D_MODEL), tok)]
    out_shape = [jax.ShapeDtypeStruct((b, 2, t, D_MODEL), BF16),
                 jax.ShapeDtypeStruct((b, t, D_MODEL), BF16)]
    args = (x, mod, nw, w_bf, cmat, smat)
    if prev is not None:
        in_specs = [pl.BlockSpec((1, tm, D_MODEL), tok), mod_spec,
                    pl.BlockSpec((D_MODEL, D_MODEL), const)] + in_specs
        out_specs = [pl.BlockSpec((1, tm, D_MODEL), tok)] + out_specs
        out_shape = [jax.ShapeDtypeStruct(x.shape, F32)] + out_shape
        args = tuple(prev) + args
    return pl.pallas_call(
        functools.partial(_fourier_in_kernel, after_attention=prev is not None),
        grid=(b, t // tm),
        in_specs=in_specs,
        out_specs=out_specs,
        out_shape=out_shape,
        compiler_params=_params("arbitrary", "arbitrary"),
        name="fourier_in_ctx" if is_ctx else "fourier_in",
    )(*args)


def _fourier_out_kernel(dft_ref, rev_ref, uc_ref, g_ref, x_ref, mod_ref, w_ref, o_ref, folded):
    n_pos = uc_ref.shape[1] // 2
    half = n_pos // 2
    tm = o_ref.shape[1]
    rt = rev_ref.shape[0]

    @pl.when(pl.program_id(1) == 0)
    def _fold():
        first_row = lax.broadcasted_iota(jnp.int32, (half, D_MODEL), 0) == 0
        for part, sign in ((0, 1.0), (1, -1.0)):
            base = part * n_pos
            tiles = [_dot(rev_ref[...], uc_ref[0, pl.ds(base + n_pos - (j + 1) * rt, rt), :])
                     for j in range(half // rt)]
            rev = jnp.concatenate(tiles, axis=0)
            mirror = jnp.where(first_row, 0.0, pltpu.roll(rev, 1, 0))
            folded[pl.ds(part * half, half), :] = (
                uc_ref[0, pl.ds(base, half), :].astype(F32) + sign * mirror).astype(BF16)

    f = _dot(dft_ref[...], folded[...])
    mid = uc_ref[0, half:half + 2 * SUBLANES, :].astype(F32)[0:1, :] * (n_pos ** -0.5)
    odd = (lax.broadcasted_iota(jnp.int32, (tm, D_MODEL), 0) & 1) == 1
    f = f + jnp.where(odd, -mid, mid)
    a = (f * g_ref[0].astype(F32)).astype(BF16)
    gate = mod_ref[0][:, 2 * D_MODEL:]
    o_ref[0] = x_ref[0] + gate * _dot(a, w_ref[...])


def _fourier_out(dft, rev, ucs, g, x, mod, w_bf, is_ctx):
    b, t, _ = x.shape
    tm = _tile(t, SEQ_TILE)
    tok = lambda bb, i: (bb, i, 0)
    return pl.pallas_call(
        _fourier_out_kernel,
        grid=(b, t // tm),
        in_specs=[pl.BlockSpec((tm, t), lambda bb, i: (i, 0)),
                  pl.BlockSpec(rev.shape, lambda bb, i: (0, 0)),
                  pl.BlockSpec((1, 2 * t, D_MODEL), lambda bb, i: (bb, 0, 0)),
                  pl.BlockSpec((1, tm, D_MODEL), tok),
                  pl.BlockSpec((1, tm, D_MODEL), tok),
                  pl.BlockSpec((1, 1, 3 * D_MODEL), _mod_index(is_ctx, b)),
                  pl.BlockSpec((D_MODEL, D_MODEL), lambda bb, i: (0, 0))],
        out_specs=pl.BlockSpec((1, tm, D_MODEL), tok),
        out_shape=jax.ShapeDtypeStruct(x.shape, F32),
        scratch_shapes=[pltpu.VMEM((t, D_MODEL), BF16)],
        compiler_params=_params("arbitrary", "arbitrary", vmem=VMEM_LIMIT_BIG),
        name="fourier_out_ctx" if is_ctx else "fourier_out",
    )(dft, rev, ucs.reshape(b, 2 * t, D_MODEL), g, x, mod, w_bf)


def _conv_in_kernel(x_ref, mod_ref, nw_ref, w_ref, u_ref, g_ref):
    h = _modnorm(x_ref[0], mod_ref[0], nw_ref[...])
    a = _dot(h, w_ref[:, :D_MODEL])
    gl = _dot(h, w_ref[:, D_MODEL:2 * D_MODEL])
    u_ref[0] = a * _sigmoid(gl)
    g_ref[0] = _silu(_dot(h, w_ref[:, 2 * D_MODEL:])).astype(BF16)


def _conv_in(x, mod, nw, w_bf, is_ctx):
    b, t, _ = x.shape
    tm = _tile(t)
    tok = lambda bb, i: (bb, i, 0)
    const = lambda bb, i: (0, 0)
    return pl.pallas_call(
        _conv_in_kernel,
        grid=(b, t // tm),
        in_specs=[pl.BlockSpec((1, tm, D_MODEL), tok),
                  pl.BlockSpec((1, 1, 3 * D_MODEL), _mod_index(is_ctx, b)),
                  pl.BlockSpec((1, D_MODEL), const),
                  pl.BlockSpec((D_MODEL, 3 * D_MODEL), const)],
        out_specs=[pl.BlockSpec((1, tm, D_MODEL), tok),
                   pl.BlockSpec((1, tm, D_MODEL), tok)],
        out_shape=[jax.ShapeDtypeStruct((b, t, D_MODEL), F32),
                   jax.ShapeDtypeStruct((b, t, D_MODEL), BF16)],
        compiler_params=_params("arbitrary", "arbitrary"),
        name="conv_in_ctx" if is_ctx else "conv_in",
    )(x, mod, nw, w_bf)


def _conv_out_kernel(u_ref, g_ref, x_ref, mod_ref, dw_ref, dwb_ref, lnw_ref, lnb_ref, w_ref,
                     o_ref, upad, conv):
    t = u_ref.shape[1]
    tm = o_ref.shape[1]
    halo = CONV_HALO
    n_cols = D_MODEL // LANES
    i = pl.program_id(1)

    @pl.when(i == 0)
    def _stage():
        for col in range(n_cols):
            upad[col, 0:halo, :] = jnp.zeros((halo, LANES), F32)
            upad[col, halo:halo + t, :] = u_ref[0, :, col * LANES:(col + 1) * LANES]
            upad[col, halo + t:, :] = jnp.zeros((halo, LANES), F32)

    groups = CONV_BLOCK_ROWS // SUBLANES
    row0 = pl.multiple_of(i * tm, SUBLANES)

    def block(n, carry):
        col = n % n_cols
        base = pl.multiple_of((n // n_cols) * CONV_BLOCK_ROWS, CONV_BLOCK_ROWS)
        acc = jnp.zeros((groups, SUBLANES, LANES), F32)
        for k in range(CONV_WIDTH):
            rows = upad[col, pl.ds(row0 + base + (halo - CONV_PAD + k), CONV_BLOCK_ROWS), :]
            acc = acc + rows.reshape(groups, SUBLANES, LANES) * dw_ref[k, col][None]
        conv[col, pl.ds(base, CONV_BLOCK_ROWS), :] = acc.reshape(CONV_BLOCK_ROWS, LANES)
        return carry

    lax.fori_loop(0, (tm // CONV_BLOCK_ROWS) * n_cols, block, 0, unroll=2)

    c = jnp.concatenate([conv[col] for col in range(n_cols)], axis=1) + dwb_ref[...]
    mu = jnp.mean(c, axis=-1, keepdims=True)
    cen = c - mu
    var = jnp.mean(cen * cen, axis=-1, keepdims=True)
    ln = cen * lax.rsqrt(var + EPS) * lnw_ref[...] + lnb_ref[...]
    a = (_silu(ln) * g_ref[0].astype(F32)).astype(BF16)
    gate = mod_ref[0][:, 2 * D_MODEL:]
    o_ref[0] = x_ref[0] + gate * _dot(a, w_ref[...])


def _conv_out(u, g, x, mod, dw_w, dw_b, ln_w, ln_b, w_bf, is_ctx):
    b, t, _ = x.shape
    tm = _tile(t, SEQ_TILE)
    dw_tiles = jnp.broadcast_to(dw_w.reshape(CONV_WIDTH, D_MODEL // LANES, 1, LANES),
                                (CONV_WIDTH, D_MODEL // LANES, SUBLANES, LANES))
    tok = lambda bb, i: (bb, i, 0)
    const = lambda bb, i: (0, 0)
    return pl.pallas_call(
        _conv_out_kernel,
        grid=(b, t // tm),
        in_specs=[pl.BlockSpec((1, t, D_MODEL), lambda bb, i: (bb, 0, 0)),
                  pl.BlockSpec((1, tm, D_MODEL), tok),
                  pl.BlockSpec((1, tm, D_MODEL), tok),
                  pl.BlockSpec((1, 1, 3 * D_MODEL), _mod_index(is_ctx, b)),
                  pl.BlockSpec((CONV_WIDTH, D_MODEL // LANES, SUBLANES, LANES),
                               lambda bb, i: (0, 0, 0, 0)),
                  pl.BlockSpec((1, D_MODEL), const),
                  pl.BlockSpec((1, D_MODEL), const),
                  pl.BlockSpec((1, D_MODEL), const),
                  pl.BlockSpec((D_MODEL, D_MODEL), const)],
        out_specs=pl.BlockSpec((1, tm, D_MODEL), tok),
        out_shape=jax.ShapeDtypeStruct(x.shape, F32),
        scratch_shapes=[pltpu.VMEM((D_MODEL // LANES, t + 2 * CONV_HALO, LANES), F32),
                        pltpu.VMEM((D_MODEL // LANES, tm, LANES), F32)],
        compiler_params=_params("arbitrary", "arbitrary", vmem=VMEM_LIMIT_BIG),
        name="conv_out_ctx" if is_ctx else "conv_out",
    )(u, g, x, mod, dw_tiles, dw_b.reshape(1, -1), ln_w.reshape(1, -1), ln_b.reshape(1, -1), w_bf)


def _rope_tables(n_tokens):
    pos = jnp.arange(n_tokens)
    pos2 = jnp.stack([pos // GRID_W, pos % GRID_W], axis=-1).astype(F32)
    inv_freq = ROPE_THETA ** (-jnp.arange(ROPE_FREQS, dtype=F32) / ROPE_FREQS)
    ang = pos2[:, :, None] * inv_freq
    cos, sin = jnp.cos(ang), jnp.sin(ang)
    cos64 = jnp.concatenate([cos[:, 0], cos[:, 0], cos[:, 1], cos[:, 1]], axis=-1)
    sin64 = jnp.concatenate([-sin[:, 0], sin[:, 0], -sin[:, 1], sin[:, 1]], axis=-1)
    return jnp.tile(cos64, (1, 2)), jnp.tile(sin64, (1, 2))


def _head_mean_matrix():
    idx = np.arange(MXU_DIM) // HEAD_DIM
    return jnp.asarray((idx[:, None] == idx[None, :]).astype(np.float32) / HEAD_DIM, dtype=BF16)


def kernel(x, c, ctx, c_ctx, l0_ada_w, l0_ada_b, l0_norm_w, l0_w_in, l0_q_norm, l0_k_norm, l0_w_out, l1_ada_w, l1_ada_b, l1_norm_w, l1_w_in, l1_w_out, l2_ada_w, l2_ada_b, l2_norm_w, l2_w_in, l2_dw_w, l2_dw_b, l2_ln_w, l2_ln_b, l2_w_out, l3_ada_w, l3_ada_b, l3_norm_w, l3_w_in, l3_q_norm, l3_k_norm, l3_w_out):
    n_batch, seq, _ = x.shape
    n_ctx = ctx.shape[1]

    rows = -(-(n_batch + 1) // SUBLANES) * SUBLANES
    cc = jnp.concatenate([c, c_ctx[None, :], jnp.zeros((rows - n_batch - 1, D_MODEL), F32)], axis=0)

    def mods(ada_w, ada_b):
        return _ada(cc, ada_w, ada_b).reshape(rows, 1, 3 * D_MODEL)

    cos, sin = _rope_tables(seq)
    cos_ctx = jnp.ones((n_ctx, LANES), F32)
    sin_ctx = jnp.zeros((n_ctx, LANES), F32)
    e_mat = _head_mean_matrix()

    def attention_layer(xl, xc, ada_w, ada_b, nw, w_in, qn, kn, w_out, need_ctx, defer_out=False):
        mod = mods(ada_w, ada_b)
        nw2 = nw.reshape(1, -1)
        w_bf = w_in.astype(BF16)
        wo_bf = w_out.astype(BF16)
        qn_t = jnp.tile(qn, N_HEADS).reshape(1, -1)
        kn_t = jnp.tile(kn, N_KV_HEADS).reshape(1, -1)
        q, k, v, g = _attn_in(xl, mod, nw2, w_bf, e_mat, qn_t, kn_t, cos, sin, False)
        ctx_out = _attn_in(xc, mod, nw2, w_bf, e_mat, qn_t, kn_t, cos_ctx, sin_ctx, True,
                           keys_only=not need_ctx)
        kc, vc = (ctx_out[1], ctx_out[2]) if need_ctx else ctx_out
        a = _attention(q, g, [(kc, vc), (k, v)], "attention")
        if defer_out:
            ac = _attention(ctx_out[0], ctx_out[3], [(kc, vc)], "attention_ctx")
            return (a, mod, wo_bf), (ac, mod, wo_bf)
        xl_new = _proj_out(a, xl, mod, wo_bf, False, "attn_out")
        if need_ctx:
            ac = _attention(ctx_out[0], ctx_out[3], [(kc, vc)], "attention_ctx")
            xc = _proj_out(ac, xc, mod, wo_bf, True, "attn_out_ctx")
        return xl_new, xc

    def fourier_layer(xl, xc, ada_w, ada_b, nw, w_in, w_out, prev_l=None, prev_c=None):
        mod = mods(ada_w, ada_b)
        nw2 = nw.reshape(1, -1)
        w_bf = w_in.astype(BF16)
        wo_bf = w_out.astype(BF16)
        cch, sch = _dft_mats(FOURIER_GROUP_DIM, FOURIER_GROUP_DIM ** -0.5)
        cch, sch = jnp.asarray(cch, dtype=BF16), jnp.asarray(sch, dtype=BF16)
        outs = []
        for t_arr, is_ctx, prev in ((xl, False, prev_l), (xc, True, prev_c)):
            t = t_arr.shape[1]
            cp, sp = _dft_mats(t, t ** -0.5)
            dft = jnp.asarray(np.concatenate([cp[:, :t // 2], -sp[:, :t // 2]], axis=1), dtype=BF16)
            rt = min(MXU_DIM, t // 2)
            rev = jnp.asarray(np.eye(rt)[::-1], dtype=BF16)
            res = _fourier_in(t_arr, mod, nw2, w_bf, cch, sch, is_ctx, prev)
            if prev is not None:
                t_arr, res = res[0], res[1:]
            ucs, g = res
            outs.append(_fourier_out(dft, rev, ucs, g, t_arr, mod, wo_bf, is_ctx))
        return outs[0], outs[1]

    def conv_layer(xl, xc, ada_w, ada_b, nw, w_in, dw_w, dw_b, ln_w, ln_b, w_out):
        mod = mods(ada_w, ada_b)
        nw2 = nw.reshape(1, -1)
        w_bf = w_in.astype(BF16)
        wo_bf = w_out.astype(BF16)
        outs = []
        for t_arr, is_ctx in ((xl, False), (xc, True)):
            u, g = _conv_in(t_arr, mod, nw2, w_bf, is_ctx)
            outs.append(_conv_out(u, g, t_arr, mod, dw_w, dw_b, ln_w, ln_b, wo_bf, is_ctx))
        return outs[0], outs[1]

    xl, xc = x, ctx
    prev_l, prev_c = attention_layer(xl, xc, l0_ada_w, l0_ada_b, l0_norm_w, l0_w_in, l0_q_norm,
                                     l0_k_norm, l0_w_out, True, defer_out=True)
    xl, xc = fourier_layer(xl, xc, l1_ada_w, l1_ada_b, l1_norm_w, l1_w_in, l1_w_out, prev_l, prev_c)
    xl, xc = conv_layer(xl, xc, l2_ada_w, l2_ada_b, l2_norm_w, l2_w_in, l2_dw_w, l2_dw_b,
                        l2_ln_w, l2_ln_b, l2_w_out)
    xl, xc = attention_layer(xl, xc, l3_ada_w, l3_ada_b, l3_norm_w, l3_w_in, l3_q_norm, l3_k_norm,
                             l3_w_out, False)
    return xl
```

```python
import functools
import math

import numpy as np
import jax
import jax.numpy as jnp
from jax import lax
from jax.experimental import pallas as pl
from jax.experimental.pallas import tpu as pltpu

D_MODEL = 1024
N_HEADS = 16
N_KV_HEADS = 4
HEAD_DIM = 64
GQA_GROUP = N_HEADS // N_KV_HEADS
Q_W = N_HEADS * HEAD_DIM
KV_W = N_KV_HEADS * HEAD_DIM
GRID_W = 64
ROPE_FREQS = HEAD_DIM // 4
ROPE_THETA = 10000.0
FOURIER_GROUP_DIM = 256
CONV_WIDTH = 31
CONV_PAD = CONV_WIDTH // 2
EPS = 1e-6

LANES = 128
SUBLANES = 8
MXU_DIM = 256
VMEM_LIMIT = 48 * 1024 * 1024
VMEM_LIMIT_BIG = 56 * 1024 * 1024

F32 = jnp.float32
BF16 = jnp.bfloat16

TOKEN_TILE = 1024
SEQ_TILE = 256
Q_TILE = 256
CONV_HALO = 2 * SUBLANES
CONV_BLOCK_ROWS = 64


def _params(*sem, vmem=VMEM_LIMIT):
    return pltpu.CompilerParams(dimension_semantics=sem, vmem_limit_bytes=vmem)


def _dot(a, b):
    return jnp.dot(a, b, preferred_element_type=F32)


def _silu(t):
    return t / (1.0 + jnp.exp(-t))


def _sigmoid(t):
    return 1.0 / (1.0 + jnp.exp(-t))


def _split_bf16(t):
    hi = t.astype(BF16)
    lo = (t - hi.astype(F32)).astype(BF16)
    return hi, lo


def _ada_kernel(cc_ref, w_ref, b_ref, o_ref):
    a_hi, a_lo = _split_bf16(_silu(cc_ref[...]))
    w_hi, w_lo = _split_bf16(w_ref[...])
    o_ref[...] = _dot(a_hi, w_hi) + (_dot(a_hi, w_lo) + _dot(a_lo, w_hi)) + b_ref[...]


def _ada(cc, ada_w, ada_b):
    rows = cc.shape[0]
    n = ada_w.shape[1]
    tn = D_MODEL
    return pl.pallas_call(
        _ada_kernel,
        grid=(n // tn,),
        in_specs=[pl.BlockSpec((rows, D_MODEL), lambda j: (0, 0)),
                  pl.BlockSpec((D_MODEL, tn), lambda j: (0, j)),
                  pl.BlockSpec((1, tn), lambda j: (0, j))],
        out_specs=pl.BlockSpec((rows, tn), lambda j: (0, j)),
        out_shape=jax.ShapeDtypeStruct((rows, n), F32),
        compiler_params=_params("arbitrary"),
        name="ada",
    )(cc, ada_w, ada_b.reshape(1, n))


def _modnorm(x, mod, nw):
    r = lax.rsqrt(jnp.mean(x * x, axis=-1, keepdims=True) + EPS)
    shift = mod[:, :D_MODEL]
    scale = mod[:, D_MODEL:2 * D_MODEL]
    return ((x * r) * (nw * (1.0 + scale)) + shift).astype(BF16)


def _mod_index(is_ctx, n_batch):
    if is_ctx:
        return lambda b, i: (n_batch, 0, 0)
    return lambda b, i: (b, 0, 0)


def _tile(t, rows=TOKEN_TILE):
    return min(rows, t)


def _head_rms(t, e_ref, w):
    outs = []
    for c in range(t.shape[1] // MXU_DIM):
        tc = t[:, c * MXU_DIM:(c + 1) * MXU_DIM]
        ms = _dot((tc * tc).astype(BF16), e_ref[...])
        outs.append(tc * lax.rsqrt(ms + EPS))
    return jnp.concatenate(outs, axis=1) * w


def _rope(t, cos, sin):
    lane = lax.broadcasted_iota(jnp.int32, (t.shape[0], LANES), 1)
    low_half = (lane & ROPE_FREQS) == 0
    outs = []
    for c in range(t.shape[1] // LANES):
        tc = t[:, c * LANES:(c + 1) * LANES]
        partner = jnp.where(low_half,
                            pltpu.roll(tc, LANES - ROPE_FREQS, 1),
                            pltpu.roll(tc, ROPE_FREQS, 1))
        outs.append(tc * cos + partner * sin)
    return jnp.concatenate(outs, axis=1)


def _attn_in_kernel(x_ref, mod_ref, nw_ref, w_ref, e_ref, qn_ref, kn_ref, cos_ref, sin_ref,
                    *out_refs, keys_only):
    h = _modnorm(x_ref[0], mod_ref[0], nw_ref[...])
    cos = cos_ref[...]
    sin = sin_ref[...]
    if keys_only:
        k_ref, v_ref = out_refs
    else:
        q_ref, k_ref, v_ref, g_ref = out_refs
        q = _head_rms(_dot(h, w_ref[:, :Q_W]), e_ref, qn_ref[...])
        q_ref[0] = (_rope(q, cos, sin) * (HEAD_DIM ** -0.5)).astype(BF16)
    k = _head_rms(_dot(h, w_ref[:, Q_W:Q_W + KV_W]), e_ref, kn_ref[...])
    k_ref[0] = _rope(k, cos, sin).astype(BF16)
    if not keys_only:
        g_ref[0] = _silu(_dot(h, w_ref[:, Q_W + 2 * KV_W:])).astype(BF16)
    v_ref[0] = _dot(h, w_ref[:, Q_W + KV_W:Q_W + 2 * KV_W]).astype(BF16)


def _attn_in(x, mod, nw, w_bf, e_mat, qn, kn, cos, sin, is_ctx, keys_only=False):
    b, t, _ = x.shape
    tm = _tile(t)
    n_in = w_bf.shape[1]
    tok = lambda bb, i: (bb, i, 0)
    const = lambda bb, i: (0, 0)
    widths = (KV_W, KV_W) if keys_only else (Q_W, KV_W, KV_W, Q_W)
    return pl.pallas_call(
        functools.partial(_attn_in_kernel, keys_only=keys_only),
        grid=(b, t // tm),
        in_specs=[pl.BlockSpec((1, tm, D_MODEL), tok),
                  pl.BlockSpec((1, 1, 3 * D_MODEL), _mod_index(is_ctx, b)),
                  pl.BlockSpec((1, D_MODEL), const),
                  pl.BlockSpec((D_MODEL, n_in), const),
                  pl.BlockSpec((MXU_DIM, MXU_DIM), const),
                  pl.BlockSpec((1, Q_W), const),
                  pl.BlockSpec((1, KV_W), const),
                  pl.BlockSpec((tm, LANES), lambda bb, i: (i, 0)),
                  pl.BlockSpec((tm, LANES), lambda bb, i: (i, 0))],
        out_specs=[pl.BlockSpec((1, tm, w), tok) for w in widths],
        out_shape=[jax.ShapeDtypeStruct((b, t, w), BF16) for w in widths],
        compiler_params=_params("arbitrary", "arbitrary"),
        name="attn_in_ctx" if is_ctx else "attn_in",
    )(x, mod, nw, w_bf, e_mat, qn, kn, cos, sin)


V_ROWS = HEAD_DIM + 16

def _attn_kernel(*refs, lens, n_qtiles, n_units, tq):
    n_src = len(lens)
    q_ref, g_ref = refs[0], refs[1]
    k_refs = refs[2:2 + n_src]
    v_refs = refs[2 + n_src:2 + 2 * n_src]
    o_ref = refs[2 + 2 * n_src]
    k_lo, k_hi, v_t, s_buf, m_buf, o_buf = refs[3 + 2 * n_src:]
    m = pl.program_id(0)
    n_keys = sum(lens)
    half = LANES // 2
    units_per_batch = N_KV_HEADS * n_qtiles
    first = jnp.minimum(2 * m, n_units - 2)
    new_batch = (first % units_per_batch == 0) & (2 * m < n_units)

    @pl.when(m == 0)
    def _fill():
        s_buf[1] = jnp.zeros(s_buf.shape[1:], F32)
        m_buf[1] = jnp.zeros(m_buf.shape[1:], F32)
        o_buf[...] = jnp.ones(o_buf.shape, F32)

    @pl.when(new_batch)
    def _stage():
        zeros = jnp.zeros((n_keys, half), BF16)
        row = lax.broadcasted_iota(jnp.int32, (V_ROWS - HEAD_DIM, n_keys), 0)
        ones_row = jnp.where(row == 0, 1.0, 0.0).astype(BF16)
        v_slot = (first // units_per_batch) % 2
        vts = [v_refs[s][0].astype(F32).T.astype(BF16) for s in range(n_src)]
        for gg in range(N_KV_HEADS):
            k_lo[gg, :, half:] = zeros
            k_hi[gg, :, :half] = zeros
            v_t[v_slot, gg, HEAD_DIM:, :] = ones_row
            off = 0
            for s in range(n_src):
                kk = k_refs[s][0, :, gg * HEAD_DIM:(gg + 1) * HEAD_DIM]
                k_lo[gg, off:off + lens[s], :half] = kk
                k_hi[gg, off:off + lens[s], half:] = kk
                v_t[v_slot, gg, :HEAD_DIM, off:off + lens[s]] = vts[s][gg * HEAD_DIM:(gg + 1) * HEAD_DIM, :]
                off += lens[s]

    def col_max(t):
        while t.shape[0] % (2 * SUBLANES) == 0:
            n = t.shape[0] // 2
            t = jnp.maximum(t[:n, :], t[n:, :])
        return jnp.max(t, axis=0, keepdims=True)

    def half_step(part):
        write_slot, read_slot = part, 1 - part
        rows = slice(part * tq, (part + 1) * tq)
        grp = (first // n_qtiles) % N_KV_HEADS
        weighed = jnp.clip(2 * m + part - 1, 0, n_units - 1)
        vt_g = v_t[(weighed // units_per_batch) % 2, (weighed // n_qtiles) % N_KV_HEADS]

        def score(h):
            q2 = q_ref[0, rows, (h // 2) * LANES:(h // 2 + 1) * LANES]
            k_st = k_hi if h % 2 else k_lo
            s_t = lax.dot_general(k_st[grp], q2, (((1,), (1,)), ((), ())),
                                  preferred_element_type=F32)
            s_buf[write_slot, h] = s_t
            m_buf[write_slot, h] = col_max(s_t)

        def weigh(h):
            p_t = jnp.exp(s_buf[read_slot, h] - m_buf[read_slot, h]).astype(BF16)
            o_buf[read_slot, h] = _dot(vt_g, p_t)

        o_rows = [o_buf[write_slot, h, :HEAD_DIM, :] / o_buf[write_slot, h, HEAD_DIM:HEAD_DIM + 1, :]
                  for h in range(GQA_GROUP)]
        o = jnp.concatenate(o_rows, axis=0).T
        o_ref[0, rows, :] = (o * g_ref[0, rows, :].astype(F32)).astype(BF16)

        for pair in range(GQA_GROUP // 2):
            score(2 * pair)
            weigh(2 * pair)
            weigh(2 * pair + 1)
            score(2 * pair + 1)

    half_step(0)
    half_step(1)


def _attention(q, g, kvs, name):
    b, t, _ = q.shape
    tq = min(Q_TILE, t // 2)
    n_qtiles = t // tq
    n_units = b * N_KV_HEADS * n_qtiles
    n_pairs = n_units // 2
    lens = tuple(int(k.shape[1]) for k, _ in kvs)
    n_keys = sum(lens)

    def pair_block(p):
        u = 2 * p
        return (u // (N_KV_HEADS * n_qtiles), (u % n_qtiles) // 2, (u // n_qtiles) % N_KV_HEADS)

    def pair_batch(p):
        return ((2 * p) // (N_KV_HEADS * n_qtiles), 0, 0)

    scored = lambda m: jnp.minimum(m, n_pairs - 1)
    stored = lambda m: jnp.maximum(m - 1, 0)
    width = GQA_GROUP * HEAD_DIM
    in_specs = [pl.BlockSpec((1, 2 * tq, width), lambda m: pair_block(scored(m))),
                pl.BlockSpec((1, 2 * tq, width), lambda m: pair_block(stored(m)))]
    in_specs += [pl.BlockSpec((1, ln, KV_W), lambda m: pair_batch(scored(m))) for ln in lens] * 2
    args = [q, g] + [k for k, _ in kvs] + [v for _, v in kvs]
    return pl.pallas_call(
        functools.partial(_attn_kernel, lens=lens, n_qtiles=n_qtiles, n_units=n_units, tq=tq),
        grid=(n_pairs + 1,),
        in_specs=in_specs,
        out_specs=pl.BlockSpec((1, 2 * tq, width), lambda m: pair_block(stored(m))),
        out_shape=jax.ShapeDtypeStruct((b, t, Q_W), BF16),
        scratch_shapes=[pltpu.VMEM((N_KV_HEADS, n_keys, LANES), BF16),
                        pltpu.VMEM((N_KV_HEADS, n_keys, LANES), BF16),
                        pltpu.VMEM((2, N_KV_HEADS, V_ROWS, n_keys), BF16),
                        pltpu.VMEM((2, GQA_GROUP, n_keys, tq), F32),
                        pltpu.VMEM((2, GQA_GROUP, 1, tq), F32),
                        pltpu.VMEM((2, GQA_GROUP, V_ROWS, tq), F32)],
        compiler_params=_params("arbitrary", vmem=VMEM_LIMIT_BIG),
        name=name,
    )(*args)


def _proj_out_kernel(a_ref, x_ref, mod_ref, w_ref, o_ref):
    gate = mod_ref[0][:, 2 * D_MODEL:]
    o_ref[0] = x_ref[0] + gate * _dot(a_ref[0], w_ref[...])


def _proj_out(a, x, mod, w_bf, is_ctx, name):
    b, t, _ = x.shape
    tm = _tile(t)
    tok = lambda bb, i: (bb, i, 0)
    return pl.pallas_call(
        _proj_out_kernel,
        grid=(b, t // tm),
        in_specs=[pl.BlockSpec((1, tm, D_MODEL), tok),
                  pl.BlockSpec((1, tm, D_MODEL), tok),
                  pl.BlockSpec((1, 1, 3 * D_MODEL), _mod_index(is_ctx, b)),
                  pl.BlockSpec((D_MODEL, D_MODEL), lambda bb, i: (0, 0))],
        out_specs=pl.BlockSpec((1, tm, D_MODEL), tok),
        out_shape=jax.ShapeDtypeStruct(x.shape, F32),
        compiler_params=_params("arbitrary", "arbitrary"),
        name=name,
    )(a, x, mod, w_bf)


def _dft_mats(n, scale):
    idx = np.arange(n, dtype=np.int64)
    ang = 2.0 * np.pi * ((idx[:, None] * idx[None, :]) % n).astype(np.float64) / n
    return np.cos(ang) * scale, np.sin(ang) * scale


def _fourier_in_kernel(*refs, after_attention):
    if after_attention:
        a_ref, pmod_ref, pw_ref = refs[:3]
        x_ref, mod_ref, nw_ref, w_ref, cc_ref, sc_ref, xo_ref, uc_ref, g_ref = refs[3:]
        x = x_ref[0] + pmod_ref[0][:, 2 * D_MODEL:] * _dot(a_ref[0], pw_ref[...])
        xo_ref[0] = x
    else:
        x_ref, mod_ref, nw_ref, w_ref, cc_ref, sc_ref, uc_ref, g_ref = refs
        x = x_ref[0]
    h = _modnorm(x, mod_ref[0], nw_ref[...])
    u = _dot(h, w_ref[:, :D_MODEL]).astype(BF16)
    ucs, uss = [], []
    for c in range(D_MODEL // FOURIER_GROUP_DIM):
        ug = u[:, c * FOURIER_GROUP_DIM:(c + 1) * FOURIER_GROUP_DIM]
        ucs.append(_dot(ug, cc_ref[...]))
        uss.append(_dot(ug, sc_ref[...]))
    uc_ref[0, 0] = jnp.concatenate(ucs, axis=1).astype(BF16)
    uc_ref[0, 1] = jnp.concatenate(uss, axis=1).astype(BF16)
    g_ref[0] = _silu(_dot(h, w_ref[:, D_MODEL:])).astype(BF16)


def _fourier_in(x, mod, nw, w_bf, cmat, smat, is_ctx, prev=None):
    b, t, _ = x.shape
    tm = _tile(t)
    tok = lambda bb, i: (bb, i, 0)
    const = lambda bb, i: (0, 0)
    mod_spec = pl.BlockSpec((1, 1, 3 * D_MODEL), _mod_index(is_ctx, b))
    in_specs = [pl.BlockSpec((1, tm, D_MODEL), tok),
                mod_spec,
                pl.BlockSpec((1, D_MODEL), const),
                pl.BlockSpec((D_MODEL, 2 * D_MODEL), const),
                pl.BlockSpec((FOURIER_GROUP_DIM, FOURIER_GROUP_DIM), const),
                pl.BlockSpec((FOURIER_GROUP_DIM, FOURIER_GROUP_DIM), const)]
    out_specs = [pl.BlockSpec((1, 2, tm, D_MODEL), lambda bb, i: (bb, 0, i, 0)),
                 pl.BlockSpec((1, tm, D_MODEL), tok)]
    out_shape = [jax.ShapeDtypeStruct((b, 2, t, D_MODEL), BF16),
                 jax.ShapeDtypeStruct((b, t, D_MODEL), BF16)]
    args = (x, mod, nw, w_bf, cmat, smat)
    if prev is not None:
        in_specs = [pl.BlockSpec((1, tm, D_MODEL), tok), mod_spec,
                    pl.BlockSpec((D_MODEL, D_MODEL), const)] + in_specs
        out_specs = [pl.BlockSpec((1, tm, D_MODEL), tok)] + out_specs
        out_shape = [jax.ShapeDtypeStruct(x.shape, F32)] + out_shape
        args = tuple(prev) + args
    return pl.pallas_call(
        functools.partial(_fourier_in_kernel, after_attention=prev is not None),
        grid=(b, t // tm),
        in_specs=in_specs,
        out_specs=out_specs,
        out_shape=out_shape,
        compiler_params=_params("arbitrary", "arbitrary"),
        name="fourier_in_ctx" if is_ctx else "fourier_in",
    )(*args)


def _fourier_out_kernel(dft_ref, rev_ref, uc_ref, g_ref, x_ref, mod_ref, w_ref, o_ref, folded):
    n_pos = uc_ref.shape[1] // 2
    half = n_pos // 2
    tm = o_ref.shape[1]
    rt = rev_ref.shape[0]

    @pl.when(pl.program_id(1) == 0)
    def _fold():
        first_row = lax.broadcasted_iota(jnp.int32, (half, D_MODEL), 0) == 0
        for part, sign in ((0, 1.0), (1, -1.0)):
            base = part * n_pos
            tiles = [_dot(rev_ref[...], uc_ref[0, pl.ds(base + n_pos - (j + 1) * rt, rt), :])
                     for j in range(half // rt)]
            rev = jnp.concatenate(tiles, axis=0)
            mirror = jnp.where(first_row, 0.0, pltpu.roll(rev, 1, 0))
            folded[pl.ds(part * half, half), :] = (
                uc_ref[0, pl.ds(base, half), :].astype(F32) + sign * mirror).astype(BF16)

    f = _dot(dft_ref[...], folded[...])
    mid = uc_ref[0, half:half + 2 * SUBLANES, :].astype(F32)[0:1, :] * (n_pos ** -0.5)
    odd = (lax.broadcasted_iota(jnp.int32, (tm, D_MODEL), 0) & 1) == 1
    f = f + jnp.where(odd, -mid, mid)
    a = (f * g_ref[0].astype(F32)).astype(BF16)
    gate = mod_ref[0][:, 2 * D_MODEL:]
    o_ref[0] = x_ref[0] + gate * _dot(a, w_ref[...])


def _fourier_out(dft, rev, ucs, g, x, mod, w_bf, is_ctx):
    b, t, _ = x.shape
    tm = _tile(t, SEQ_TILE)
    tok = lambda bb, i: (bb, i, 0)
    return pl.pallas_call(
        _fourier_out_kernel,
        grid=(b, t // tm),
        in_specs=[pl.BlockSpec((tm, t), lambda bb, i: (i, 0)),
                  pl.BlockSpec(rev.shape, lambda bb, i: (0, 0)),
                  pl.BlockSpec((1, 2 * t, D_MODEL), lambda bb, i: (bb, 0, 0)),
                  pl.BlockSpec((1, tm, D_MODEL), tok),
                  pl.BlockSpec((1, tm, D_MODEL), tok),
                  pl.BlockSpec((1, 1, 3 * D_MODEL), _mod_index(is_ctx, b)),
                  pl.BlockSpec((D_MODEL, D_MODEL), lambda bb, i: (0, 0))],
        out_specs=pl.BlockSpec((1, tm, D_MODEL), tok),
        out_shape=jax.ShapeDtypeStruct(x.shape, F32),
        scratch_shapes=[pltpu.VMEM((t, D_MODEL), BF16)],
        compiler_params=_params("arbitrary", "arbitrary", vmem=VMEM_LIMIT_BIG),
        name="fourier_out_ctx" if is_ctx else "fourier_out",
    )(dft, rev, ucs.reshape(b, 2 * t, D_MODEL), g, x, mod, w_bf)


def _conv_in_kernel(x_ref, mod_ref, nw_ref, w_ref, u_ref, g_ref):
    h = _modnorm(x_ref[0], mod_ref[0], nw_ref[...])
    a = _dot(h, w_ref[:, :D_MODEL])
    gl = _dot(h, w_ref[:, D_MODEL:2 * D_MODEL])
    u_ref[0] = a * _sigmoid(gl)
    g_ref[0] = _silu(_dot(h, w_ref[:, 2 * D_MODEL:])).astype(BF16)


def _conv_in(x, mod, nw, w_bf, is_ctx):
    b, t, _ = x.shape
    tm = _tile(t)
    tok = lambda bb, i: (bb, i, 0)
    const = lambda bb, i: (0, 0)
    return pl.pallas_call(
        _conv_in_kernel,
        grid=(b, t // tm),
        in_specs=[pl.BlockSpec((1, tm, D_MODEL), tok),
                  pl.BlockSpec((1, 1, 3 * D_MODEL), _mod_index(is_ctx, b)),
                  pl.BlockSpec((1, D_MODEL), const),
                  pl.BlockSpec((D_MODEL, 3 * D_MODEL), const)],
        out_specs=[pl.BlockSpec((1, tm, D_MODEL), tok),
                   pl.BlockSpec((1, tm, D_MODEL), tok)],
        out_shape=[jax.ShapeDtypeStruct((b, t, D_MODEL), F32),
                   jax.ShapeDtypeStruct((b, t, D_MODEL), BF16)],
        compiler_params=_params("arbitrary", "arbitrary"),
        name="conv_in_ctx" if is_ctx else "conv_in",
    )(x, mod, nw, w_bf)


def _conv_out_kernel(u_ref, g_ref, x_ref, mod_ref, dw_ref, dwb_ref, lnw_ref, lnb_ref, w_ref,
                     o_ref, upad, conv):
    t = u_ref.shape[1]
    tm = o_ref.shape[1]
    halo = CONV_HALO
    n_cols = D_MODEL // LANES
    i = pl.program_id(1)

    @pl.when(i == 0)
    def _stage():
        for col in range(n_cols):
            upad[col, 0:halo, :] = jnp.zeros((halo, LANES), F32)
            upad[col, halo:halo + t, :] = u_ref[0, :, col * LANES:(col + 1) * LANES]
            upad[col, halo + t:, :] = jnp.zeros((halo, LANES), F32)

    groups = CONV_BLOCK_ROWS // SUBLANES
    row0 = pl.multiple_of(i * tm, SUBLANES)

    def block(n, carry):
        col = n % n_cols
        base = pl.multiple_of((n // n_cols) * CONV_BLOCK_ROWS, CONV_BLOCK_ROWS)
        acc = jnp.zeros((groups, SUBLANES, LANES), F32)
        for k in range(CONV_WIDTH):
            rows = upad[col, pl.ds(row0 + base + (halo - CONV_PAD + k), CONV_BLOCK_ROWS), :]
            acc = acc + rows.reshape(groups, SUBLANES, LANES) * dw_ref[k, col][None]
        conv[col, pl.ds(base, CONV_BLOCK_ROWS), :] = acc.reshape(CONV_BLOCK_ROWS, LANES)
        return carry

    lax.fori_loop(0, (tm // CONV_BLOCK_ROWS) * n_cols, block, 0, unroll=2)

    c = jnp.concatenate([conv[col] for col in range(n_cols)], axis=1) + dwb_ref[...]
    mu = jnp.mean(c, axis=-1, keepdims=True)
    cen = c - mu
    var = jnp.mean(cen * cen, axis=-1, keepdims=True)
    ln = cen * lax.rsqrt(var + EPS) * lnw_ref[...] + lnb_ref[...]
    a = (_silu(ln) * g_ref[0].astype(F32)).astype(BF16)
    gate = mod_ref[0][:, 2 * D_MODEL:]
    o_ref[0] = x_ref[0] + gate * _dot(a, w_ref[...])


def _conv_out(u, g, x, mod, dw_w, dw_b, ln_w, ln_b, w_bf, is_ctx):
    b, t, _ = x.shape
    tm = _tile(t, SEQ_TILE)
    dw_tiles = jnp.broadcast_to(dw_w.reshape(CONV_WIDTH, D_MODEL // LANES, 1, LANES),
                                (CONV_WIDTH, D_MODEL // LANES, SUBLANES, LANES))
    tok = lambda bb, i: (bb, i, 0)
    const = lambda bb, i: (0, 0)
    return pl.pallas_call(
        _conv_out_kernel,
        grid=(b, t // tm),
        in_specs=[pl.BlockSpec((1, t, D_MODEL), lambda bb, i: (bb, 0, 0)),
                  pl.BlockSpec((1, tm, D_MODEL), tok),
                  pl.BlockSpec((1, tm, D_MODEL), tok),
                  pl.BlockSpec((1, 1, 3 * D_MODEL), _mod_index(is_ctx, b)),
                  pl.BlockSpec((CONV_WIDTH, D_MODEL // LANES, SUBLANES, LANES),
                               lambda bb, i: (0, 0, 0, 0)),
                  pl.BlockSpec((1, D_MODEL), const),
                  pl.BlockSpec((1, D_MODEL), const),
                  pl.BlockSpec((1, D_MODEL), const),
                  pl.BlockSpec((D_MODEL, D_MODEL), const)],
        out_specs=pl.BlockSpec((1, tm, D_MODEL), tok),
        out_shape=jax.ShapeDtypeStruct(x.shape, F32),
        scratch_shapes=[pltpu.VMEM((D_MODEL // LANES, t + 2 * CONV_HALO, LANES), F32),
                        pltpu.VMEM((D_MODEL // LANES, tm, LANES), F32)],
        compiler_params=_params("arbitrary", "arbitrary", vmem=VMEM_LIMIT_BIG),
        name="conv_out_ctx" if is_ctx else "conv_out",
    )(u, g, x, mod, dw_tiles, dw_b.reshape(1, -1), ln_w.reshape(1, -1), ln_b.reshape(1, -1), w_bf)


def _rope_tables(n_tokens):
    pos = jnp.arange(n_tokens)
    pos2 = jnp.stack([pos // GRID_W, pos % GRID_W], axis=-1).astype(F32)
    inv_freq = ROPE_THETA ** (-jnp.arange(ROPE_FREQS, dtype=F32) / ROPE_FREQS)
    ang = pos2[:, :, None] * inv_freq
    cos, sin = jnp.cos(ang), jnp.sin(ang)
    cos64 = jnp.concatenate([cos[:, 0], cos[:, 0], cos[:, 1], cos[:, 1]], axis=-1)
    sin64 = jnp.concatenate([-sin[:, 0], sin[:, 0], -sin[:, 1], sin[:, 1]], axis=-1)
    return jnp.tile(cos64, (1, 2)), jnp.tile(sin64, (1, 2))


def _head_mean_matrix():
    idx = np.arange(MXU_DIM) // HEAD_DIM
    return jnp.asarray((idx[:, None] == idx[None, :]).astype(np.float32) / HEAD_DIM, dtype=BF16)


def kernel(x, c, ctx, c_ctx, l0_ada_w, l0_ada_b, l0_norm_w, l0_w_in, l0_q_norm, l0_k_norm, l0_w_out, l1_ada_w, l1_ada_b, l1_norm_w, l1_w_in, l1_w_out, l2_ada_w, l2_ada_b, l2_norm_w, l2_w_in, l2_dw_w, l2_dw_b, l2_ln_w, l2_ln_b, l2_w_out, l3_ada_w, l3_ada_b, l3_norm_w, l3_w_in, l3_q_norm, l3_k_norm, l3_w_out):
    n_batch, seq, _ = x.shape
    n_ctx = ctx.shape[1]

    rows = -(-(n_batch + 1) // SUBLANES) * SUBLANES
    cc = jnp.concatenate([c, c_ctx[None, :], jnp.zeros((rows - n_batch - 1, D_MODEL), F32)], axis=0)

    def mods(ada_w, ada_b):
        return _ada(cc, ada_w, ada_b).reshape(rows, 1, 3 * D_MODEL)

    cos, sin = _rope_tables(seq)
    cos_ctx = jnp.ones((n_ctx, LANES), F32)
    sin_ctx = jnp.zeros((n_ctx, LANES), F32)
    e_mat = _head_mean_matrix()

    def attention_layer(xl, xc, ada_w, ada_b, nw, w_in, qn, kn, w_out, need_ctx, defer_out=False):
        mod = mods(ada_w, ada_b)
        nw2 = nw.reshape(1, -1)
        w_bf = w_in.astype(BF16)
        wo_bf = w_out.astype(BF16)
        qn_t = jnp.tile(qn, N_HEADS).reshape(1, -1)
        kn_t = jnp.tile(kn, N_KV_HEADS).reshape(1, -1)
        q, k, v, g = _attn_in(xl, mod, nw2, w_bf, e_mat, qn_t, kn_t, cos, sin, False)
        ctx_out = _attn_in(xc, mod, nw2, w_bf, e_mat, qn_t, kn_t, cos_ctx, sin_ctx, True,
                           keys_only=not need_ctx)
        kc, vc = (ctx_out[1], ctx_out[2]) if need_ctx else ctx_out
        a = _attention(q, g, [(kc, vc), (k, v)], "attention")
        if defer_out:
            ac = _attention(ctx_out[0], ctx_out[3], [(kc, vc)], "attention_ctx")
            return (a, mod, wo_bf), (ac, mod, wo_bf)
        xl_new = _proj_out(a, xl, mod, wo_bf, False, "attn_out")
        if need_ctx:
            ac = _attention(ctx_out[0], ctx_out[3], [(kc, vc)], "attention_ctx")
            xc = _proj_out(ac, xc, mod, wo_bf, True, "attn_out_ctx")
        return xl_new, xc

    def fourier_layer(xl, xc, ada_w, ada_b, nw, w_in, w_out, prev_l=None, prev_c=None):
        mod = mods(ada_w, ada_b)
        nw2 = nw.reshape(1, -1)
        w_bf = w_in.astype(BF16)
        wo_bf = w_out.astype(BF16)
        cch, sch = _dft_mats(FOURIER_GROUP_DIM, FOURIER_GROUP_DIM ** -0.5)
        cch, sch = jnp.asarray(cch, dtype=BF16), jnp.asarray(sch, dtype=BF16)
        outs = []
        for t_arr, is_ctx, prev in ((xl, False, prev_l), (xc, True, prev_c)):
            t = t_arr.shape[1]
            cp, sp = _dft_mats(t, t ** -0.5)
            dft = jnp.asarray(np.concatenate([cp[:, :t // 2], -sp[:, :t // 2]], axis=1), dtype=BF16)
            rt = min(MXU_DIM, t // 2)
            rev = jnp.asarray(np.eye(rt)[::-1], dtype=BF16)
            res = _fourier_in(t_arr, mod, nw2, w_bf, cch, sch, is_ctx, prev)
            if prev is not None:
                t_arr, res = res[0], res[1:]
            ucs, g = res
            outs.append(_fourier_out(dft, rev, ucs, g, t_arr, mod, wo_bf, is_ctx))
        return outs[0], outs[1]

    def conv_layer(xl, xc, ada_w, ada_b, nw, w_in, dw_w, dw_b, ln_w, ln_b, w_out):
        mod = mods(ada_w, ada_b)
        nw2 = nw.reshape(1, -1)
        w_bf = w_in.astype(BF16)
        wo_bf = w_out.astype(BF16)
        outs = []
        for t_arr, is_ctx in ((xl, False), (xc, True)):
            u, g = _conv_in(t_arr, mod, nw2, w_bf, is_ctx)
            outs.append(_conv_out(u, g, t_arr, mod, dw_w, dw_b, ln_w, ln_b, wo_bf, is_ctx))
        return outs[0], outs[1]

    xl, xc = x, ctx
    prev_l, prev_c = attention_layer(xl, xc, l0_ada_w, l0_ada_b, l0_norm_w, l0_w_in, l0_q_norm,
                                     l0_k_norm, l0_w_out, True, defer_out=True)
    xl, xc = fourier_layer(xl, xc, l1_ada_w, l1_ada_b, l1_norm_w, l1_w_in, l1_w_out, prev_l, prev_c)
    xl, xc = conv_layer(xl, xc, l2_ada_w, l2_ada_b, l2_norm_w, l2_w_in, l2_dw_w, l2_dw_b,
                        l2_ln_w, l2_ln_b, l2_w_out)
    xl, xc = attention_layer(xl, xc, l3_ada_w, l3_ada_b, l3_norm_w, l3_w_in, l3_q_norm, l3_k_norm,
                             l3_w_out, False)
    return xl
```

```python
import functools
import math

import numpy as np
import jax
import jax.numpy as jnp
from jax import lax
from jax.experimental import pallas as pl
from jax.experimental.pallas import tpu as pltpu

D_MODEL = 1024
N_HEADS = 16
N_KV_HEADS = 4
HEAD_DIM = 64
GQA_GROUP = N_HEADS // N_KV_HEADS
Q_W = N_HEADS * HEAD_DIM
KV_W = N_KV_HEADS * HEAD_DIM
GRID_W = 64
ROPE_FREQS = HEAD_DIM // 4
ROPE_THETA = 10000.0
FOURIER_GROUP_DIM = 256
CONV_WIDTH = 31
CONV_PAD = CONV_WIDTH // 2
EPS = 1e-6

LANES = 128
SUBLANES = 8
MXU_DIM = 256
VMEM_LIMIT = 48 * 1024 * 1024
VMEM_LIMIT_BIG = 56 * 1024 * 1024

F32 = jnp.float32
BF16 = jnp.bfloat16

TOKEN_TILE = 1024
SEQ_TILE = 256
Q_TILE = 256
ATTN_UNITS_PER_STEP = 4
CONV_HALO = 2 * SUBLANES
CONV_BLOCK_ROWS = 64


def _params(*sem, vmem=VMEM_LIMIT):
    return pltpu.CompilerParams(dimension_semantics=sem, vmem_limit_bytes=vmem)


def _dot(a, b):
    return jnp.dot(a, b, preferred_element_type=F32)


def _silu(t):
    return t / (1.0 + jnp.exp(-t))


def _sigmoid(t):
    return 1.0 / (1.0 + jnp.exp(-t))


def _split_bf16(t):
    hi = t.astype(BF16)
    lo = (t - hi.astype(F32)).astype(BF16)
    return hi, lo


def _ada_kernel(cc_ref, w_ref, b_ref, o_ref):
    a_hi, a_lo = _split_bf16(_silu(cc_ref[...]))
    w_hi, w_lo = _split_bf16(w_ref[...])
    o_ref[...] = _dot(a_hi, w_hi) + (_dot(a_hi, w_lo) + _dot(a_lo, w_hi)) + b_ref[...]


def _ada(cc, ada_w, ada_b):
    rows = cc.shape[0]
    n = ada_w.shape[1]
    tn = D_MODEL
    return pl.pallas_call(
        _ada_kernel,
        grid=(n // tn,),
        in_specs=[pl.BlockSpec((rows, D_MODEL), lambda j: (0, 0)),
                  pl.BlockSpec((D_MODEL, tn), lambda j: (0, j)),
                  pl.BlockSpec((1, tn), lambda j: (0, j))],
        out_specs=pl.BlockSpec((rows, tn), lambda j: (0, j)),
        out_shape=jax.ShapeDtypeStruct((rows, n), F32),
        compiler_params=_params("arbitrary"),
        name="ada",
    )(cc, ada_w, ada_b.reshape(1, n))


def _modnorm(x, mod, nw):
    r = lax.rsqrt(jnp.mean(x * x, axis=-1, keepdims=True) + EPS)
    shift = mod[:, :D_MODEL]
    scale = mod[:, D_MODEL:2 * D_MODEL]
    return ((x * r) * (nw * (1.0 + scale)) + shift).astype(BF16)


def _mod_index(is_ctx, n_batch):
    if is_ctx:
        return lambda b, i: (n_batch, 0, 0)
    return lambda b, i: (b, 0, 0)


def _tile(t, rows=TOKEN_TILE):
    return min(rows, t)


def _head_rms(t, e_ref, w):
    outs = []
    for c in range(t.shape[1] // MXU_DIM):
        tc = t[:, c * MXU_DIM:(c + 1) * MXU_DIM]
        ms = _dot((tc * tc).astype(BF16), e_ref[...])
        outs.append(tc * lax.rsqrt(ms + EPS))
    return jnp.concatenate(outs, axis=1) * w


def _rope(t, cos, sin):
    lane = lax.broadcasted_iota(jnp.int32, (t.shape[0], LANES), 1)
    low_half = (lane & ROPE_FREQS) == 0
    outs = []
    for c in range(t.shape[1] // LANES):
        tc = t[:, c * LANES:(c + 1) * LANES]
        partner = jnp.where(low_half,
                            pltpu.roll(tc, LANES - ROPE_FREQS, 1),
                            pltpu.roll(tc, ROPE_FREQS, 1))
        outs.append(tc * cos + partner * sin)
    return jnp.concatenate(outs, axis=1)


def _attn_in_kernel(x_ref, mod_ref, nw_ref, w_ref, e_ref, qn_ref, kn_ref, cos_ref, sin_ref,
                    *out_refs, keys_only):
    h = _modnorm(x_ref[0], mod_ref[0], nw_ref[...])
    cos = cos_ref[...]
    sin = sin_ref[...]
    if keys_only:
        k_ref, v_ref = out_refs
    else:
        q_ref, k_ref, v_ref, g_ref = out_refs
        q = _head_rms(_dot(h, w_ref[:, :Q_W]), e_ref, qn_ref[...])
        q_ref[0] = (_rope(q, cos, sin) * (HEAD_DIM ** -0.5)).astype(BF16)
    k = _head_rms(_dot(h, w_ref[:, Q_W:Q_W + KV_W]), e_ref, kn_ref[...])
    k_ref[0] = _rope(k, cos, sin).astype(BF16)
    if not keys_only:
        g_ref[0] = _silu(_dot(h, w_ref[:, Q_W + 2 * KV_W:])).astype(BF16)
    v_ref[0] = _dot(h, w_ref[:, Q_W + KV_W:Q_W + 2 * KV_W]).astype(BF16)


def _attn_in(x, mod, nw, w_bf, e_mat, qn, kn, cos, sin, is_ctx, keys_only=False):
    b, t, _ = x.shape
    tm = _tile(t)
    n_in = w_bf.shape[1]
    tok = lambda bb, i: (bb, i, 0)
    const = lambda bb, i: (0, 0)
    widths = (KV_W, KV_W) if keys_only else (Q_W, KV_W, KV_W, Q_W)
    return pl.pallas_call(
        functools.partial(_attn_in_kernel, keys_only=keys_only),
        grid=(b, t // tm),
        in_specs=[pl.BlockSpec((1, tm, D_MODEL), tok),
                  pl.BlockSpec((1, 1, 3 * D_MODEL), _mod_index(is_ctx, b)),
                  pl.BlockSpec((1, D_MODEL), const),
                  pl.BlockSpec((D_MODEL, n_in), const),
                  pl.BlockSpec((MXU_DIM, MXU_DIM), const),
                  pl.BlockSpec((1, Q_W), const),
                  pl.BlockSpec((1, KV_W), const),
                  pl.BlockSpec((tm, LANES), lambda bb, i: (i, 0)),
                  pl.BlockSpec((tm, LANES), lambda bb, i: (i, 0))],
        out_specs=[pl.BlockSpec((1, tm, w), tok) for w in widths],
        out_shape=[jax.ShapeDtypeStruct((b, t, w), BF16) for w in widths],
        compiler_params=_params("arbitrary", "arbitrary"),
        name="attn_in_ctx" if is_ctx else "attn_in",
    )(x, mod, nw, w_bf, e_mat, qn, kn, cos, sin)


V_ROWS = HEAD_DIM + 16

def _attn_kernel(*refs, lens, n_qtiles, n_units, tq, per_step):
    n_src = len(lens)
    q_ref, g_ref = refs[0], refs[1]
    k_refs = refs[2:2 + n_src]
    v_refs = refs[2 + n_src:2 + 2 * n_src]
    o_ref = refs[2 + 2 * n_src]
    k_lo, k_hi, v_t, s_buf, m_buf, o_buf = refs[3 + 2 * n_src:]
    m = pl.program_id(0)
    n_keys = sum(lens)
    half = LANES // 2
    units_per_batch = N_KV_HEADS * n_qtiles
    first = jnp.minimum(per_step * m, n_units - per_step)
    new_batch = (first % units_per_batch == 0) & (per_step * m < n_units)

    @pl.when(m == 0)
    def _fill():
        s_buf[1] = jnp.zeros(s_buf.shape[1:], F32)
        m_buf[1] = jnp.zeros(m_buf.shape[1:], F32)
        o_buf[...] = jnp.ones(o_buf.shape, F32)

    @pl.when(new_batch)
    def _stage():
        zeros = jnp.zeros((n_keys, half), BF16)
        row = lax.broadcasted_iota(jnp.int32, (V_ROWS - HEAD_DIM, n_keys), 0)
        ones_row = jnp.where(row == 0, 1.0, 0.0).astype(BF16)
        v_slot = (first // units_per_batch) % 2
        vts = [v_refs[s][0].astype(F32).T.astype(BF16) for s in range(n_src)]
        for gg in range(N_KV_HEADS):
            k_lo[gg, :, half:] = zeros
            k_hi[gg, :, :half] = zeros
            v_t[v_slot, gg, HEAD_DIM:, :] = ones_row
            off = 0
            for s in range(n_src):
                kk = k_refs[s][0, :, gg * HEAD_DIM:(gg + 1) * HEAD_DIM]
                k_lo[gg, off:off + lens[s], :half] = kk
                k_hi[gg, off:off + lens[s], half:] = kk
                v_t[v_slot, gg, :HEAD_DIM, off:off + lens[s]] = vts[s][gg * HEAD_DIM:(gg + 1) * HEAD_DIM, :]
                off += lens[s]

    def col_max(t):
        while t.shape[0] % (2 * SUBLANES) == 0:
            n = t.shape[0] // 2
            t = jnp.maximum(t[:n, :], t[n:, :])
        return jnp.max(t, axis=0, keepdims=True)

    def sub_step(j):
        write_slot, read_slot = j % 2, (j - 1) % 2
        acc_w, acc_r = (j - 1) % per_step, j
        rows = slice(j * tq, (j + 1) * tq)
        grp = (first // n_qtiles) % N_KV_HEADS
        weighed = jnp.clip(per_step * m + j - 1, 0, n_units - 1)
        vt_g = v_t[(weighed // units_per_batch) % 2, (weighed // n_qtiles) % N_KV_HEADS]

        def score(h):
            q2 = q_ref[0, rows, (h // 2) * LANES:(h // 2 + 1) * LANES]
            k_st = k_hi if h % 2 else k_lo
            s_t = lax.dot_general(k_st[grp], q2, (((1,), (1,)), ((), ())),
                                  preferred_element_type=F32)
            s_buf[write_slot, h] = s_t
            m_buf[write_slot, h] = col_max(s_t)

        def weigh(h):
            p_t = jnp.exp(s_buf[read_slot, h] - m_buf[read_slot, h]).astype(BF16)
            o_buf[acc_w, h] = _dot(vt_g, p_t)

        o_rows = [o_buf[acc_r, h, :HEAD_DIM, :] / o_buf[acc_r, h, HEAD_DIM:HEAD_DIM + 1, :]
                  for h in range(GQA_GROUP)]
        o = jnp.concatenate(o_rows, axis=0).T
        o_ref[0, rows, :] = (o * g_ref[0, rows, :].astype(F32)).astype(BF16)

        for pair in range(GQA_GROUP // 2):
            score(2 * pair)
            weigh(2 * pair)
            weigh(2 * pair + 1)
            score(2 * pair + 1)

    for j in range(per_step):
        sub_step(j)


def _attention(q, g, kvs, name):
    b, t, _ = q.shape
    per_step = max(2, min(ATTN_UNITS_PER_STEP, t // Q_TILE))
    tq = t // max(per_step, t // Q_TILE)
    n_qtiles = t // tq
    n_units = b * N_KV_HEADS * n_qtiles
    n_steps = n_units // per_step
    lens = tuple(int(k.shape[1]) for k, _ in kvs)
    n_keys = sum(lens)

    def step_block(s):
        u = per_step * s
        return (u // (N_KV_HEADS * n_qtiles), (u % n_qtiles) // per_step, (u // n_qtiles) % N_KV_HEADS)

    def step_batch(s):
        return ((per_step * s) // (N_KV_HEADS * n_qtiles), 0, 0)

    scored = lambda m: jnp.minimum(m, n_steps - 1)
    stored = lambda m: jnp.maximum(m - 1, 0)
    width = GQA_GROUP * HEAD_DIM
    in_specs = [pl.BlockSpec((1, per_step * tq, width), lambda m: step_block(scored(m))),
                pl.BlockSpec((1, per_step * tq, width), lambda m: step_block(stored(m)))]
    in_specs += [pl.BlockSpec((1, ln, KV_W), lambda m: step_batch(scored(m))) for ln in lens] * 2
    args = [q, g] + [k for k, _ in kvs] + [v for _, v in kvs]
    return pl.pallas_call(
        functools.partial(_attn_kernel, lens=lens, n_qtiles=n_qtiles, n_units=n_units, tq=tq,
                          per_step=per_step),
        grid=(n_steps + 1,),
        in_specs=in_specs,
        out_specs=pl.BlockSpec((1, per_step * tq, width), lambda m: step_block(stored(m))),
        out_shape=jax.ShapeDtypeStruct((b, t, Q_W), BF16),
        scratch_shapes=[pltpu.VMEM((N_KV_HEADS, n_keys, LANES), BF16),
                        pltpu.VMEM((N_KV_HEADS, n_keys, LANES), BF16),
                        pltpu.VMEM((2, N_KV_HEADS, V_ROWS, n_keys), BF16),
                        pltpu.VMEM((2, GQA_GROUP, n_keys, tq), F32),
                        pltpu.VMEM((2, GQA_GROUP, 1, tq), F32),
                        pltpu.VMEM((per_step, GQA_GROUP, V_ROWS, tq), F32)],
        compiler_params=_params("arbitrary", vmem=VMEM_LIMIT_BIG),
        name=name,
    )(*args)


def _proj_out_kernel(a_ref, x_ref, mod_ref, w_ref, o_ref):
    gate = mod_ref[0][:, 2 * D_MODEL:]
    o_ref[0] = x_ref[0] + gate * _dot(a_ref[0], w_ref[...])


def _proj_out(a, x, mod, w_bf, is_ctx, name):
    b, t, _ = x.shape
    tm = _tile(t)
    tok = lambda bb, i: (bb, i, 0)
    return pl.pallas_call(
        _proj_out_kernel,
        grid=(b, t // tm),
        in_specs=[pl.BlockSpec((1, tm, D_MODEL), tok),
                  pl.BlockSpec((1, tm, D_MODEL), tok),
                  pl.BlockSpec((1, 1, 3 * D_MODEL), _mod_index(is_ctx, b)),
                  pl.BlockSpec((D_MODEL, D_MODEL), lambda bb, i: (0, 0))],
        out_specs=pl.BlockSpec((1, tm, D_MODEL), tok),
        out_shape=jax.ShapeDtypeStruct(x.shape, F32),
        compiler_params=_params("arbitrary", "arbitrary"),
        name=name,
    )(a, x, mod, w_bf)


def _dft_mats(n, scale):
    idx = np.arange(n, dtype=np.int64)
    ang = 2.0 * np.pi * ((idx[:, None] * idx[None, :]) % n).astype(np.float64) / n
    return np.cos(ang) * scale, np.sin(ang) * scale


def _fourier_in_kernel(*refs, after_attention):
    if after_attention:
        a_ref, pmod_ref, pw_ref = refs[:3]
        x_ref, mod_ref, nw_ref, w_ref, cc_ref, sc_ref, xo_ref, uc_ref, g_ref = refs[3:]
        x = x_ref[0] + pmod_ref[0][:, 2 * D_MODEL:] * _dot(a_ref[0], pw_ref[...])
        xo_ref[0] = x
    else:
        x_ref, mod_ref, nw_ref, w_ref, cc_ref, sc_ref, uc_ref, g_ref = refs
        x = x_ref[0]
    h = _modnorm(x, mod_ref[0], nw_ref[...])
    u = _dot(h, w_ref[:, :D_MODEL]).astype(BF16)
    ucs, uss = [], []
    for c in range(D_MODEL // FOURIER_GROUP_DIM):
        ug = u[:, c * FOURIER_GROUP_DIM:(c + 1) * FOURIER_GROUP_DIM]
        ucs.append(_dot(ug, cc_ref[...]))
        uss.append(_dot(ug, sc_ref[...]))
    uc_ref[0, 0] = jnp.concatenate(ucs, axis=1).astype(BF16)
    uc_ref[0, 1] = jnp.concatenate(uss, axis=1).astype(BF16)
    g_ref[0] = _silu(_dot(h, w_ref[:, D_MODEL:])).astype(BF16)


def _fourier_in(x, mod, nw, w_bf, cmat, smat, is_ctx, prev=None):
    b, t, _ = x.shape
    tm = _tile(t)
    tok = lambda bb, i: (bb, i, 0)
    const = lambda bb, i: (0, 0)
    mod_spec = pl.BlockSpec((1, 1, 3 * D_MODEL), _mod_index(is_ctx, b))
    in_specs = [pl.BlockSpec((1, tm, D_MODEL), tok),
                mod_spec,
                pl.BlockSpec((1, D_MODEL), const),
                pl.BlockSpec((D_MODEL, 2 * D_MODEL), const),
                pl.BlockSpec((FOURIER_GROUP_DIM, FOURIER_GROUP_DIM), const),
                pl.BlockSpec((FOURIER_GROUP_DIM, FOURIER_GROUP_DIM), const)]
    out_specs = [pl.BlockSpec((1, 2, tm, D_MODEL), lambda bb, i: (bb, 0, i, 0)),
                 pl.BlockSpec((1, tm, D_MODEL), tok)]
    out_shape = [jax.ShapeDtypeStruct((b, 2, t, D_MODEL), BF16),
                 jax.ShapeDtypeStruct((b, t, D_MODEL), BF16)]
    args = (x, mod, nw, w_bf, cmat, smat)
    if prev is not None:
        in_specs = [pl.BlockSpec((1, tm, D_MODEL), tok), mod_spec,
                    pl.BlockSpec((D_MODEL, D_MODEL), const)] + in_specs
        out_specs = [pl.BlockSpec((1, tm, D_MODEL), tok)] + out_specs
        out_shape = [jax.ShapeDtypeStruct(x.shape, F32)] + out_shape
        args = tuple(prev) + args
    return pl.pallas_call(
        functools.partial(_fourier_in_kernel, after_attention=prev is not None),
        grid=(b, t // tm),
        in_specs=in_specs,
        out_specs=out_specs,
        out_shape=out_shape,
        compiler_params=_params("arbitrary", "arbitrary"),
        name="fourier_in_ctx" if is_ctx else "fourier_in",
    )(*args)


def _fourier_out_kernel(dft_ref, rev_ref, uc_ref, g_ref, x_ref, mod_ref, w_ref, o_ref, folded):
    n_pos = uc_ref.shape[1] // 2
    half = n_pos // 2
    tm = o_ref.shape[1]
    rt = rev_ref.shape[0]

    @pl.when(pl.program_id(1) == 0)
    def _fold():
        first_row = lax.broadcasted_iota(jnp.int32, (half, D_MODEL), 0) == 0
        for part, sign in ((0, 1.0), (1, -1.0)):
            base = part * n_pos
            tiles = [_dot(rev_ref[...], uc_ref[0, pl.ds(base + n_pos - (j + 1) * rt, rt), :])
                     for j in range(half // rt)]
            rev = jnp.concatenate(tiles, axis=0)
            mirror = jnp.where(first_row, 0.0, pltpu.roll(rev, 1, 0))
            folded[pl.ds(part * half, half), :] = (
                uc_ref[0, pl.ds(base, half), :].astype(F32) + sign * mirror).astype(BF16)

    f = _dot(dft_ref[...], folded[...])
    mid = uc_ref[0, half:half + 2 * SUBLANES, :].astype(F32)[0:1, :] * (n_pos ** -0.5)
    odd = (lax.broadcasted_iota(jnp.int32, (tm, D_MODEL), 0) & 1) == 1
    f = f + jnp.where(odd, -mid, mid)
    a = (f * g_ref[0].astype(F32)).astype(BF16)
    gate = mod_ref[0][:, 2 * D_MODEL:]
    o_ref[0] = x_ref[0] + gate * _dot(a, w_ref[...])


def _fourier_out(dft, rev, ucs, g, x, mod, w_bf, is_ctx):
    b, t, _ = x.shape
    tm = _tile(t, SEQ_TILE)
    tok = lambda bb, i: (bb, i, 0)
    return pl.pallas_call(
        _fourier_out_kernel,
        grid=(b, t // tm),
        in_specs=[pl.BlockSpec((tm, t), lambda bb, i: (i, 0)),
                  pl.BlockSpec(rev.shape, lambda bb, i: (0, 0)),
                  pl.BlockSpec((1, 2 * t, D_MODEL), lambda bb, i: (bb, 0, 0)),
                  pl.BlockSpec((1, tm, D_MODEL), tok),
                  pl.BlockSpec((1, tm, D_MODEL), tok),
                  pl.BlockSpec((1, 1, 3 * D_MODEL), _mod_index(is_ctx, b)),
                  pl.BlockSpec((D_MODEL, D_MODEL), lambda bb, i: (0, 0))],
        out_specs=pl.BlockSpec((1, tm, D_MODEL), tok),
        out_shape=jax.ShapeDtypeStruct(x.shape, F32),
        scratch_shapes=[pltpu.VMEM((t, D_MODEL), BF16)],
        compiler_params=_params("arbitrary", "arbitrary", vmem=VMEM_LIMIT_BIG),
        name="fourier_out_ctx" if is_ctx else "fourier_out",
    )(dft, rev, ucs.reshape(b, 2 * t, D_MODEL), g, x, mod, w_bf)


def _conv_in_kernel(x_ref, mod_ref, nw_ref, w_ref, u_ref, g_ref):
    h = _modnorm(x_ref[0], mod_ref[0], nw_ref[...])
    a = _dot(h, w_ref[:, :D_MODEL])
    gl = _dot(h, w_ref[:, D_MODEL:2 * D_MODEL])
    u_ref[0] = a * _sigmoid(gl)
    g_ref[0] = _silu(_dot(h, w_ref[:, 2 * D_MODEL:])).astype(BF16)


def _conv_in(x, mod, nw, w_bf, is_ctx):
    b, t, _ = x.shape
    tm = _tile(t)
    tok = lambda bb, i: (bb, i, 0)
    const = lambda bb, i: (0, 0)
    return pl.pallas_call(
        _conv_in_kernel,
        grid=(b, t // tm),
        in_specs=[pl.BlockSpec((1, tm, D_MODEL), tok),
                  pl.BlockSpec((1, 1, 3 * D_MODEL), _mod_index(is_ctx, b)),
                  pl.BlockSpec((1, D_MODEL), const),
                  pl.BlockSpec((D_MODEL, 3 * D_MODEL), const)],
        out_specs=[pl.BlockSpec((1, tm, D_MODEL), tok),
                   pl.BlockSpec((1, tm, D_MODEL), tok)],
        out_shape=[jax.ShapeDtypeStruct((b, t, D_MODEL), F32),
                   jax.ShapeDtypeStruct((b, t, D_MODEL), BF16)],
        compiler_params=_params("arbitrary", "arbitrary"),
        name="conv_in_ctx" if is_ctx else "conv_in",
    )(x, mod, nw, w_bf)


def _conv_out_kernel(u_ref, g_ref, x_ref, mod_ref, dw_ref, dwb_ref, lnw_ref, lnb_ref, w_ref,
                     o_ref, upad, conv):
    t = u_ref.shape[1]
    tm = o_ref.shape[1]
    halo = CONV_HALO
    n_cols = D_MODEL // LANES
    i = pl.program_id(1)

    @pl.when(i == 0)
    def _stage():
        for col in range(n_cols):
            upad[col, 0:halo, :] = jnp.zeros((halo, LANES), F32)
            upad[col, halo:halo + t, :] = u_ref[0, :, col * LANES:(col + 1) * LANES]
            upad[col, halo + t:, :] = jnp.zeros((halo, LANES), F32)

    groups = CONV_BLOCK_ROWS // SUBLANES
    row0 = pl.multiple_of(i * tm, SUBLANES)

    def block(n, carry):
        col = n % n_cols
        base = pl.multiple_of((n // n_cols) * CONV_BLOCK_ROWS, CONV_BLOCK_ROWS)
        acc = jnp.zeros((groups, SUBLANES, LANES), F32)
        for k in range(CONV_WIDTH):
            rows = upad[col, pl.ds(row0 + base + (halo - CONV_PAD + k), CONV_BLOCK_ROWS), :]
            acc = acc + rows.reshape(groups, SUBLANES, LANES) * dw_ref[k, col][None]
        conv[col, pl.ds(base, CONV_BLOCK_ROWS), :] = acc.reshape(CONV_BLOCK_ROWS, LANES)
        return carry

    lax.fori_loop(0, (tm // CONV_BLOCK_ROWS) * n_cols, block, 0, unroll=2)

    c = jnp.concatenate([conv[col] for col in range(n_cols)], axis=1) + dwb_ref[...]
    mu = jnp.mean(c, axis=-1, keepdims=True)
    cen = c - mu
    var = jnp.mean(cen * cen, axis=-1, keepdims=True)
    ln = cen * lax.rsqrt(var + EPS) * lnw_ref[...] + lnb_ref[...]
    a = (_silu(ln) * g_ref[0].astype(F32)).astype(BF16)
    gate = mod_ref[0][:, 2 * D_MODEL:]
    o_ref[0] = x_ref[0] + gate * _dot(a, w_ref[...])


def _conv_out(u, g, x, mod, dw_w, dw_b, ln_w, ln_b, w_bf, is_ctx):
    b, t, _ = x.shape
    tm = _tile(t, SEQ_TILE)
    dw_tiles = jnp.broadcast_to(dw_w.reshape(CONV_WIDTH, D_MODEL // LANES, 1, LANES),
                                (CONV_WIDTH, D_MODEL // LANES, SUBLANES, LANES))
    tok = lambda bb, i: (bb, i, 0)
    const = lambda bb, i: (0, 0)
    return pl.pallas_call(
        _conv_out_kernel,
        grid=(b, t // tm),
        in_specs=[pl.BlockSpec((1, t, D_MODEL), lambda bb, i: (bb, 0, 0)),
                  pl.BlockSpec((1, tm, D_MODEL), tok),
                  pl.BlockSpec((1, tm, D_MODEL), tok),
                  pl.BlockSpec((1, 1, 3 * D_MODEL), _mod_index(is_ctx, b)),
                  pl.BlockSpec((CONV_WIDTH, D_MODEL // LANES, SUBLANES, LANES),
                               lambda bb, i: (0, 0, 0, 0)),
                  pl.BlockSpec((1, D_MODEL), const),
                  pl.BlockSpec((1, D_MODEL), const),
                  pl.BlockSpec((1, D_MODEL), const),
                  pl.BlockSpec((D_MODEL, D_MODEL), const)],
        out_specs=pl.BlockSpec((1, tm, D_MODEL), tok),
        out_shape=jax.ShapeDtypeStruct(x.shape, F32),
        scratch_shapes=[pltpu.VMEM((D_MODEL // LANES, t + 2 * CONV_HALO, LANES), F32),
                        pltpu.VMEM((D_MODEL // LANES, tm, LANES), F32)],
        compiler_params=_params("arbitrary", "arbitrary", vmem=VMEM_LIMIT_BIG),
        name="conv_out_ctx" if is_ctx else "conv_out",
    )(u, g, x, mod, dw_tiles, dw_b.reshape(1, -1), ln_w.reshape(1, -1), ln_b.reshape(1, -1), w_bf)


def _rope_tables(n_tokens):
    pos = jnp.arange(n_tokens)
    pos2 = jnp.stack([pos // GRID_W, pos % GRID_W], axis=-1).astype(F32)
    inv_freq = ROPE_THETA ** (-jnp.arange(ROPE_FREQS, dtype=F32) / ROPE_FREQS)
    ang = pos2[:, :, None] * inv_freq
    cos, sin = jnp.cos(ang), jnp.sin(ang)
    cos64 = jnp.concatenate([cos[:, 0], cos[:, 0], cos[:, 1], cos[:, 1]], axis=-1)
    sin64 = jnp.concatenate([-sin[:, 0], sin[:, 0], -sin[:, 1], sin[:, 1]], axis=-1)
    return jnp.tile(cos64, (1, 2)), jnp.tile(sin64, (1, 2))


def _head_mean_matrix():
    idx = np.arange(MXU_DIM) // HEAD_DIM
    return jnp.asarray((idx[:, None] == idx[None, :]).astype(np.float32) / HEAD_DIM, dtype=BF16)


def kernel(x, c, ctx, c_ctx, l0_ada_w, l0_ada_b, l0_norm_w, l0_w_in, l0_q_norm, l0_k_norm, l0_w_out, l1_ada_w, l1_ada_b, l1_norm_w, l1_w_in, l1_w_out, l2_ada_w, l2_ada_b, l2_norm_w, l2_w_in, l2_dw_w, l2_dw_b, l2_ln_w, l2_ln_b, l2_w_out, l3_ada_w, l3_ada_b, l3_norm_w, l3_w_in, l3_q_norm, l3_k_norm, l3_w_out):
    n_batch, seq, _ = x.shape
    n_ctx = ctx.shape[1]

    rows = -(-(n_batch + 1) // SUBLANES) * SUBLANES
    cc = jnp.concatenate([c, c_ctx[None, :], jnp.zeros((rows - n_batch - 1, D_MODEL), F32)], axis=0)

    def mods(ada_w, ada_b):
        return _ada(cc, ada_w, ada_b).reshape(rows, 1, 3 * D_MODEL)

    cos, sin = _rope_tables(seq)
    cos_ctx = jnp.ones((n_ctx, LANES), F32)
    sin_ctx = jnp.zeros((n_ctx, LANES), F32)
    e_mat = _head_mean_matrix()

    def attention_layer(xl, xc, ada_w, ada_b, nw, w_in, qn, kn, w_out, need_ctx, defer_out=False):
        mod = mods(ada_w, ada_b)
        nw2 = nw.reshape(1, -1)
        w_bf = w_in.astype(BF16)
        wo_bf = w_out.astype(BF16)
        qn_t = jnp.tile(qn, N_HEADS).reshape(1, -1)
        kn_t = jnp.tile(kn, N_KV_HEADS).reshape(1, -1)
        q, k, v, g = _attn_in(xl, mod, nw2, w_bf, e_mat, qn_t, kn_t, cos, sin, False)
        ctx_out = _attn_in(xc, mod, nw2, w_bf, e_mat, qn_t, kn_t, cos_ctx, sin_ctx, True,
                           keys_only=not need_ctx)
        kc, vc = (ctx_out[1], ctx_out[2]) if need_ctx else ctx_out
        a = _attention(q, g, [(kc, vc), (k, v)], "attention")
        if defer_out:
            ac = _attention(ctx_out[0], ctx_out[3], [(kc, vc)], "attention_ctx")
            return (a, mod, wo_bf), (ac, mod, wo_bf)
        xl_new = _proj_out(a, xl, mod, wo_bf, False, "attn_out")
        if need_ctx:
            ac = _attention(ctx_out[0], ctx_out[3], [(kc, vc)], "attention_ctx")
            xc = _proj_out(ac, xc, mod, wo_bf, True, "attn_out_ctx")
        return xl_new, xc

    def fourier_layer(xl, xc, ada_w, ada_b, nw, w_in, w_out, prev_l=None, prev_c=None):
        mod = mods(ada_w, ada_b)
        nw2 = nw.reshape(1, -1)
        w_bf = w_in.astype(BF16)
        wo_bf = w_out.astype(BF16)
        cch, sch = _dft_mats(FOURIER_GROUP_DIM, FOURIER_GROUP_DIM ** -0.5)
        cch, sch = jnp.asarray(cch, dtype=BF16), jnp.asarray(sch, dtype=BF16)
        outs = []
        for t_arr, is_ctx, prev in ((xl, False, prev_l), (xc, True, prev_c)):
            t = t_arr.shape[1]
            cp, sp = _dft_mats(t, t ** -0.5)
            dft = jnp.asarray(np.concatenate([cp[:, :t // 2], -sp[:, :t // 2]], axis=1), dtype=BF16)
            rt = min(MXU_DIM, t // 2)
            rev = jnp.asarray(np.eye(rt)[::-1], dtype=BF16)
            res = _fourier_in(t_arr, mod, nw2, w_bf, cch, sch, is_ctx, prev)
            if prev is not None:
                t_arr, res = res[0], res[1:]
            ucs, g = res
            outs.append(_fourier_out(dft, rev, ucs, g, t_arr, mod, wo_bf, is_ctx))
        return outs[0], outs[1]

    def conv_layer(xl, xc, ada_w, ada_b, nw, w_in, dw_w, dw_b, ln_w, ln_b, w_out):
        mod = mods(ada_w, ada_b)
        nw2 = nw.reshape(1, -1)
        w_bf = w_in.astype(BF16)
        wo_bf = w_out.astype(BF16)
        outs = []
        for t_arr, is_ctx in ((xl, False), (xc, True)):
            u, g = _conv_in(t_arr, mod, nw2, w_bf, is_ctx)
            outs.append(_conv_out(u, g, t_arr, mod, dw_w, dw_b, ln_w, ln_b, wo_bf, is_ctx))
        return outs[0], outs[1]

    xl, xc = x, ctx
    prev_l, prev_c = attention_layer(xl, xc, l0_ada_w, l0_ada_b, l0_norm_w, l0_w_in, l0_q_norm,
                                     l0_k_norm, l0_w_out, True, defer_out=True)
    xl, xc = fourier_layer(xl, xc, l1_ada_w, l1_ada_b, l1_norm_w, l1_w_in, l1_w_out, prev_l, prev_c)
    xl, xc = conv_layer(xl, xc, l2_ada_w, l2_ada_b, l2_norm_w, l2_w_in, l2_dw_w, l2_dw_b,
                        l2_ln_w, l2_ln_b, l2_w_out)
    xl, xc = attention_layer(xl, xc, l3_ada_w, l3_ada_b, l3_norm_w, l3_w_in, l3_q_norm, l3_k_norm,
                             l3_w_out, False)
    return xl
```

```python
import functools
import math

import numpy as np
import jax
import jax.numpy as jnp
from jax import lax
from jax.experimental import pallas as pl
from jax.experimental.pallas import tpu as pltpu

D_MODEL = 1024
N_HEADS = 16
N_KV_HEADS = 4
HEAD_DIM = 64
GQA_GROUP = N_HEADS // N_KV_HEADS
Q_W = N_HEADS * HEAD_DIM
KV_W = N_KV_HEADS * HEAD_DIM
GRID_W = 64
ROPE_FREQS = HEAD_DIM // 4
ROPE_THETA = 10000.0
FOURIER_GROUP_DIM = 256
CONV_WIDTH = 31
CONV_PAD = CONV_WIDTH // 2
EPS = 1e-6

LANES = 128
SUBLANES = 8
MXU_DIM = 256
VMEM_LIMIT = 48 * 1024 * 1024
VMEM_LIMIT_BIG = 56 * 1024 * 1024

F32 = jnp.float32
BF16 = jnp.bfloat16

TOKEN_TILE = 1024
SEQ_TILE = 256
Q_TILE = 256
ATTN_UNITS_PER_STEP = 4
CONV_HALO = 2 * SUBLANES
CONV_BLOCK_ROWS = 64


def _params(*sem, vmem=VMEM_LIMIT):
    return pltpu.CompilerParams(dimension_semantics=sem, vmem_limit_bytes=vmem)


def _dot(a, b):
    return jnp.dot(a, b, preferred_element_type=F32)


def _silu(t):
    return t / (1.0 + jnp.exp(-t))


def _sigmoid(t):
    return 1.0 / (1.0 + jnp.exp(-t))


def _split_bf16(t):
    hi = t.astype(BF16)
    lo = (t - hi.astype(F32)).astype(BF16)
    return hi, lo


def _ada_kernel(cc_ref, w_ref, b_ref, o_ref):
    a_hi, a_lo = _split_bf16(_silu(cc_ref[...]))
    w_hi, w_lo = _split_bf16(w_ref[...])
    o_ref[...] = _dot(a_hi, w_hi) + (_dot(a_hi, w_lo) + _dot(a_lo, w_hi)) + b_ref[...]


def _ada(cc, ada_w, ada_b):
    rows = cc.shape[0]
    n = ada_w.shape[1]
    tn = D_MODEL
    return pl.pallas_call(
        _ada_kernel,
        grid=(n // tn,),
        in_specs=[pl.BlockSpec((rows, D_MODEL), lambda j: (0, 0)),
                  pl.BlockSpec((D_MODEL, tn), lambda j: (0, j)),
                  pl.BlockSpec((1, tn), lambda j: (0, j))],
        out_specs=pl.BlockSpec((rows, tn), lambda j: (0, j)),
        out_shape=jax.ShapeDtypeStruct((rows, n), F32),
        compiler_params=_params("arbitrary"),
        name="ada",
    )(cc, ada_w, ada_b.reshape(1, n))


def _modnorm(x, mod, nw):
    r = lax.rsqrt(jnp.mean(x * x, axis=-1, keepdims=True) + EPS)
    shift = mod[:, :D_MODEL]
    scale = mod[:, D_MODEL:2 * D_MODEL]
    return ((x * r) * (nw * (1.0 + scale)) + shift).astype(BF16)


def _mod_index(is_ctx, n_batch):
    if is_ctx:
        return lambda b, i: (n_batch, 0, 0)
    return lambda b, i: (b, 0, 0)


def _tile(t, rows=TOKEN_TILE):
    return min(rows, t)


def _head_rms(t, e_ref, w):
    outs = []
    for c in range(t.shape[1] // MXU_DIM):
        tc = t[:, c * MXU_DIM:(c + 1) * MXU_DIM]
        ms = _dot((tc * tc).astype(BF16), e_ref[...])
        outs.append(tc * lax.rsqrt(ms + EPS))
    return jnp.concatenate(outs, axis=1) * w


def _rope(t, cos, sin):
    lane = lax.broadcasted_iota(jnp.int32, (t.shape[0], LANES), 1)
    low_half = (lane & ROPE_FREQS) == 0
    outs = []
    for c in range(t.shape[1] // LANES):
        tc = t[:, c * LANES:(c + 1) * LANES]
        partner = jnp.where(low_half,
                            pltpu.roll(tc, LANES - ROPE_FREQS, 1),
                            pltpu.roll(tc, ROPE_FREQS, 1))
        outs.append(tc * cos + partner * sin)
    return jnp.concatenate(outs, axis=1)


def _attn_in_kernel(x_ref, mod_ref, nw_ref, w_ref, e_ref, qn_ref, kn_ref, cos_ref, sin_ref,
                    *out_refs, keys_only):
    h = _modnorm(x_ref[0], mod_ref[0], nw_ref[...])
    cos = cos_ref[...]
    sin = sin_ref[...]
    if keys_only:
        k_ref, v_ref = out_refs
        k_raw = _dot(h, w_ref[:, Q_W:Q_W + KV_W])
    else:
        q_ref, k_ref, v_ref, g_ref = out_refs
        q_raw = _dot(h, w_ref[:, :Q_W])
        k_raw = _dot(h, w_ref[:, Q_W:Q_W + KV_W])
        q = _head_rms(q_raw, e_ref, qn_ref[...])
        q_ref[0] = (_rope(q, cos, sin) * (HEAD_DIM ** -0.5)).astype(BF16)
        g_ref[0] = _silu(_dot(h, w_ref[:, Q_W + 2 * KV_W:])).astype(BF16)
    v_raw = _dot(h, w_ref[:, Q_W + KV_W:Q_W + 2 * KV_W])
    k = _head_rms(k_raw, e_ref, kn_ref[...])
    k_ref[0] = _rope(k, cos, sin).astype(BF16)
    v_ref[0] = v_raw.astype(BF16)


def _attn_in(x, mod, nw, w_bf, e_mat, qn, kn, cos, sin, is_ctx, keys_only=False):
    b, t, _ = x.shape
    tm = _tile(t)
    n_in = w_bf.shape[1]
    tok = lambda bb, i: (bb, i, 0)
    const = lambda bb, i: (0, 0)
    widths = (KV_W, KV_W) if keys_only else (Q_W, KV_W, KV_W, Q_W)
    return pl.pallas_call(
        functools.partial(_attn_in_kernel, keys_only=keys_only),
        grid=(b, t // tm),
        in_specs=[pl.BlockSpec((1, tm, D_MODEL), tok),
                  pl.BlockSpec((1, 1, 3 * D_MODEL), _mod_index(is_ctx, b)),
                  pl.BlockSpec((1, D_MODEL), const),
                  pl.BlockSpec((D_MODEL, n_in), const),
                  pl.BlockSpec((MXU_DIM, MXU_DIM), const),
                  pl.BlockSpec((1, Q_W), const),
                  pl.BlockSpec((1, KV_W), const),
                  pl.BlockSpec((tm, LANES), lambda bb, i: (i, 0)),
                  pl.BlockSpec((tm, LANES), lambda bb, i: (i, 0))],
        out_specs=[pl.BlockSpec((1, tm, w), tok) for w in widths],
        out_shape=[jax.ShapeDtypeStruct((b, t, w), BF16) for w in widths],
        compiler_params=_params("arbitrary", "arbitrary"),
        name="attn_in_ctx" if is_ctx else "attn_in",
    )(x, mod, nw, w_bf, e_mat, qn, kn, cos, sin)


V_ROWS = HEAD_DIM + 16

def _attn_kernel(*refs, lens, n_qtiles, n_units, tq, per_step):
    n_src = len(lens)
    q_ref, g_ref = refs[0], refs[1]
    k_refs = refs[2:2 + n_src]
    v_refs = refs[2 + n_src:2 + 2 * n_src]
    o_ref = refs[2 + 2 * n_src]
    k_lo, k_hi, v_t, s_buf, m_buf, o_buf = refs[3 + 2 * n_src:]
    m = pl.program_id(0)
    n_keys = sum(lens)
    half = LANES // 2
    units_per_batch = N_KV_HEADS * n_qtiles
    first = jnp.minimum(per_step * m, n_units - per_step)
    new_batch = (first % units_per_batch == 0) & (per_step * m < n_units)

    @pl.when(m == 0)
    def _fill():
        s_buf[1] = jnp.zeros(s_buf.shape[1:], F32)
        m_buf[1] = jnp.zeros(m_buf.shape[1:], F32)
        o_buf[...] = jnp.ones(o_buf.shape, F32)

    @pl.when(new_batch)
    def _stage():
        zeros = jnp.zeros((n_keys, half), BF16)
        row = lax.broadcasted_iota(jnp.int32, (V_ROWS - HEAD_DIM, n_keys), 0)
        ones_row = jnp.where(row == 0, 1.0, 0.0).astype(BF16)
        v_slot = (first // units_per_batch) % 2
        vts = [v_refs[s][0].astype(F32).T.astype(BF16) for s in range(n_src)]
        for gg in range(N_KV_HEADS):
            k_lo[gg, :, half:] = zeros
            k_hi[gg, :, :half] = zeros
            v_t[v_slot, gg, HEAD_DIM:, :] = ones_row
            off = 0
            for s in range(n_src):
                kk = k_refs[s][0, :, gg * HEAD_DIM:(gg + 1) * HEAD_DIM]
                k_lo[gg, off:off + lens[s], :half] = kk
                k_hi[gg, off:off + lens[s], half:] = kk
                v_t[v_slot, gg, :HEAD_DIM, off:off + lens[s]] = vts[s][gg * HEAD_DIM:(gg + 1) * HEAD_DIM, :]
                off += lens[s]

    def col_max(t):
        while t.shape[0] % (2 * SUBLANES) == 0:
            n = t.shape[0] // 2
            t = jnp.maximum(t[:n, :], t[n:, :])
        return jnp.max(t, axis=0, keepdims=True)

    def sub_step(j):
        write_slot, read_slot = j % 2, (j - 1) % 2
        acc_w, acc_r = (j - 1) % per_step, j
        rows = slice(j * tq, (j + 1) * tq)
        grp = (first // n_qtiles) % N_KV_HEADS
        weighed = jnp.clip(per_step * m + j - 1, 0, n_units - 1)
        vt_g = v_t[(weighed // units_per_batch) % 2, (weighed // n_qtiles) % N_KV_HEADS]

        def score(h):
            q2 = q_ref[0, rows, (h // 2) * LANES:(h // 2 + 1) * LANES]
            k_st = k_hi if h % 2 else k_lo
            s_t = lax.dot_general(k_st[grp], q2, (((1,), (1,)), ((), ())),
                                  preferred_element_type=F32)
            s_buf[write_slot, h] = s_t
            m_buf[write_slot, h] = col_max(s_t)

        def weigh(h):
            p_t = jnp.exp(s_buf[read_slot, h] - m_buf[read_slot, h]).astype(BF16)
            o_buf[acc_w, h] = _dot(vt_g, p_t)

        o_rows = [o_buf[acc_r, h, :HEAD_DIM, :] / o_buf[acc_r, h, HEAD_DIM:HEAD_DIM + 1, :]
                  for h in range(GQA_GROUP)]
        o = jnp.concatenate(o_rows, axis=0).T
        o_ref[0, rows, :] = (o * g_ref[0, rows, :].astype(F32)).astype(BF16)

        for pair in range(GQA_GROUP // 2):
            score(2 * pair)
            weigh(2 * pair)
            weigh(2 * pair + 1)
            score(2 * pair + 1)

    for j in range(per_step):
        sub_step(j)


def _attention(q, g, kvs, name):
    b, t, _ = q.shape
    per_step = max(2, min(ATTN_UNITS_PER_STEP, t // Q_TILE))
    tq = t // max(per_step, t // Q_TILE)
    n_qtiles = t // tq
    n_units = b * N_KV_HEADS * n_qtiles
    n_steps = n_units // per_step
    lens = tuple(int(k.shape[1]) for k, _ in kvs)
    n_keys = sum(lens)

    def step_block(s):
        u = per_step * s
        return (u // (N_KV_HEADS * n_qtiles), (u % n_qtiles) // per_step, (u // n_qtiles) % N_KV_HEADS)

    def step_batch(s):
        return ((per_step * s) // (N_KV_HEADS * n_qtiles), 0, 0)

    scored = lambda m: jnp.minimum(m, n_steps - 1)
    stored = lambda m: jnp.maximum(m - 1, 0)
    width = GQA_GROUP * HEAD_DIM
    in_specs = [pl.BlockSpec((1, per_step * tq, width), lambda m: step_block(scored(m))),
                pl.BlockSpec((1, per_step * tq, width), lambda m: step_block(stored(m)))]
    in_specs += [pl.BlockSpec((1, ln, KV_W), lambda m: step_batch(scored(m))) for ln in lens] * 2
    args = [q, g] + [k for k, _ in kvs] + [v for _, v in kvs]
    return pl.pallas_call(
        functools.partial(_attn_kernel, lens=lens, n_qtiles=n_qtiles, n_units=n_units, tq=tq,
                          per_step=per_step),
        grid=(n_steps + 1,),
        in_specs=in_specs,
        out_specs=pl.BlockSpec((1, per_step * tq, width), lambda m: step_block(stored(m))),
        out_shape=jax.ShapeDtypeStruct((b, t, Q_W), BF16),
        scratch_shapes=[pltpu.VMEM((N_KV_HEADS, n_keys, LANES), BF16),
                        pltpu.VMEM((N_KV_HEADS, n_keys, LANES), BF16),
                        pltpu.VMEM((2, N_KV_HEADS, V_ROWS, n_keys), BF16),
                        pltpu.VMEM((2, GQA_GROUP, n_keys, tq), F32),
                        pltpu.VMEM((2, GQA_GROUP, 1, tq), F32),
                        pltpu.VMEM((per_step, GQA_GROUP, V_ROWS, tq), F32)],
        compiler_params=_params("arbitrary", vmem=VMEM_LIMIT_BIG),
        name=name,
    )(*args)


def _proj_out_kernel(a_ref, x_ref, mod_ref, w_ref, o_ref):
    gate = mod_ref[0][:, 2 * D_MODEL:]
    o_ref[0] = x_ref[0] + gate * _dot(a_ref[0], w_ref[...])


def _proj_out(a, x, mod, w_bf, is_ctx, name):
    b, t, _ = x.shape
    tm = _tile(t)
    tok = lambda bb, i: (bb, i, 0)
    return pl.pallas_call(
        _proj_out_kernel,
        grid=(b, t // tm),
        in_specs=[pl.BlockSpec((1, tm, D_MODEL), tok),
                  pl.BlockSpec((1, tm, D_MODEL), tok),
                  pl.BlockSpec((1, 1, 3 * D_MODEL), _mod_index(is_ctx, b)),
                  pl.BlockSpec((D_MODEL, D_MODEL), lambda bb, i: (0, 0))],
        out_specs=pl.BlockSpec((1, tm, D_MODEL), tok),
        out_shape=jax.ShapeDtypeStruct(x.shape, F32),
        compiler_params=_params("arbitrary", "arbitrary"),
        name=name,
    )(a, x, mod, w_bf)


def _dft_mats(n, scale):
    idx = np.arange(n, dtype=np.int64)
    ang = 2.0 * np.pi * ((idx[:, None] * idx[None, :]) % n).astype(np.float64) / n
    return np.cos(ang) * scale, np.sin(ang) * scale


def _fourier_in_kernel(*refs, after_attention):
    if after_attention:
        a_ref, pmod_ref, pw_ref = refs[:3]
        x_ref, mod_ref, nw_ref, w_ref, cc_ref, sc_ref, xo_ref, uc_ref, g_ref = refs[3:]
        x = x_ref[0] + pmod_ref[0][:, 2 * D_MODEL:] * _dot(a_ref[0], pw_ref[...])
        xo_ref[0] = x
    else:
        x_ref, mod_ref, nw_ref, w_ref, cc_ref, sc_ref, uc_ref, g_ref = refs
        x = x_ref[0]
    h = _modnorm(x, mod_ref[0], nw_ref[...])
    u = _dot(h, w_ref[:, :D_MODEL]).astype(BF16)
    g_ref[0] = _silu(_dot(h, w_ref[:, D_MODEL:])).astype(BF16)
    ucs, uss = [], []
    for c in range(D_MODEL // FOURIER_GROUP_DIM):
        ug = u[:, c * FOURIER_GROUP_DIM:(c + 1) * FOURIER_GROUP_DIM]
        ucs.append(_dot(ug, cc_ref[...]))
        uss.append(_dot(ug, sc_ref[...]))
    uc_ref[0, 0] = jnp.concatenate(ucs, axis=1).astype(BF16)
    uc_ref[0, 1] = jnp.concatenate(uss, axis=1).astype(BF16)


def _fourier_in(x, mod, nw, w_bf, cmat, smat, is_ctx, prev=None):
    b, t, _ = x.shape
    tm = _tile(t)
    tok = lambda bb, i: (bb, i, 0)
    const = lambda bb, i: (0, 0)
    mod_spec = pl.BlockSpec((1, 1, 3 * D_MODEL), _mod_index(is_ctx, b))
    in_specs = [pl.BlockSpec((1, tm, D_MODEL), tok),
                mod_spec,
                pl.BlockSpec((1, D_MODEL), const),
                pl.BlockSpec((D_MODEL, 2 * D_MODEL), const),
                pl.BlockSpec((FOURIER_GROUP_DIM, FOURIER_GROUP_DIM), const),
                pl.BlockSpec((FOURIER_GROUP_DIM, FOURIER_GROUP_DIM), const)]
    out_specs = [pl.BlockSpec((1, 2, tm, D_MODEL), lambda bb, i: (bb, 0, i, 0)),
                 pl.BlockSpec((1, tm, D_MODEL), tok)]
    out_shape = [jax.ShapeDtypeStruct((b, 2, t, D_MODEL), BF16),
                 jax.ShapeDtypeStruct((b, t, D_MODEL), BF16)]
    args = (x, mod, nw, w_bf, cmat, smat)
    if prev is not None:
        in_specs = [pl.BlockSpec((1, tm, D_MODEL), tok), mod_spec,
                    pl.BlockSpec((D_MODEL, D_MODEL), const)] + in_specs
        out_specs = [pl.BlockSpec((1, tm, D_MODEL), tok)] + out_specs
        out_shape = [jax.ShapeDtypeStruct(x.shape, F32)] + out_shape
        args = tuple(prev) + args
    return pl.pallas_call(
        functools.partial(_fourier_in_kernel, after_attention=prev is not None),
        grid=(b, t // tm),
        in_specs=in_specs,
        out_specs=out_specs,
        out_shape=out_shape,
        compiler_params=_params("arbitrary", "arbitrary"),
        name="fourier_in_ctx" if is_ctx else "fourier_in",
    )(*args)


def _fourier_out_kernel(dft_ref, rev_ref, uc_ref, g_ref, x_ref, mod_ref, w_ref, o_ref, folded):
    n_pos = uc_ref.shape[1] // 2
    half = n_pos // 2
    tm = o_ref.shape[1]
    rt = rev_ref.shape[0]

    @pl.when(pl.program_id(1) == 0)
    def _fold():
        first_row = lax.broadcasted_iota(jnp.int32, (half, D_MODEL), 0) == 0
        for part, sign in ((0, 1.0), (1, -1.0)):
            base = part * n_pos
            tiles = [_dot(rev_ref[...], uc_ref[0, pl.ds(base + n_pos - (j + 1) * rt, rt), :])
                     for j in range(half // rt)]
            rev = jnp.concatenate(tiles, axis=0)
            mirror = jnp.where(first_row, 0.0, pltpu.roll(rev, 1, 0))
            folded[pl.ds(part * half, half), :] = (
                uc_ref[0, pl.ds(base, half), :].astype(F32) + sign * mirror).astype(BF16)

    mid = uc_ref[0, half:half + 2 * SUBLANES, :].astype(F32)[0:1, :] * (n_pos ** -0.5)
    gate = mod_ref[0][:, 2 * D_MODEL:]
    chunk = min(tm, MXU_DIM)
    spans = [slice(r, r + chunk) for r in range(0, tm, chunk)]
    odd = (lax.broadcasted_iota(jnp.int32, (chunk, D_MODEL), 0) & 1) == 1
    fs = [_dot(dft_ref[rows, :], folded[...]) for rows in spans]
    for rows, f in zip(spans, fs):
        f = f + jnp.where(odd, -mid, mid)
        a = (f * g_ref[0, rows, :].astype(F32)).astype(BF16)
        o_ref[0, rows, :] = x_ref[0, rows, :] + gate * _dot(a, w_ref[...])


def _fourier_out(dft, rev, ucs, g, x, mod, w_bf, is_ctx):
    b, t, _ = x.shape
    tm = _tile(t, 2 * SEQ_TILE)
    tok = lambda bb, i: (bb, i, 0)
    return pl.pallas_call(
        _fourier_out_kernel,
        grid=(b, t // tm),
        in_specs=[pl.BlockSpec((tm, t), lambda bb, i: (i, 0)),
                  pl.BlockSpec(rev.shape, lambda bb, i: (0, 0)),
                  pl.BlockSpec((1, 2 * t, D_MODEL), lambda bb, i: (bb, 0, 0)),
                  pl.BlockSpec((1, tm, D_MODEL), tok),
                  pl.BlockSpec((1, tm, D_MODEL), tok),
                  pl.BlockSpec((1, 1, 3 * D_MODEL), _mod_index(is_ctx, b)),
                  pl.BlockSpec((D_MODEL, D_MODEL), lambda bb, i: (0, 0))],
        out_specs=pl.BlockSpec((1, tm, D_MODEL), tok),
        out_shape=jax.ShapeDtypeStruct(x.shape, F32),
        scratch_shapes=[pltpu.VMEM((t, D_MODEL), BF16)],
        compiler_params=_params("arbitrary", "arbitrary", vmem=VMEM_LIMIT_BIG),
        name="fourier_out_ctx" if is_ctx else "fourier_out",
    )(dft, rev, ucs.reshape(b, 2 * t, D_MODEL), g, x, mod, w_bf)


def _conv_in_kernel(x_ref, mod_ref, nw_ref, w_ref, u_ref, g_ref):
    h = _modnorm(x_ref[0], mod_ref[0], nw_ref[...])
    a = _dot(h, w_ref[:, :D_MODEL])
    gl = _dot(h, w_ref[:, D_MODEL:2 * D_MODEL])
    u_ref[0] = a * _sigmoid(gl)
    g_ref[0] = _silu(_dot(h, w_ref[:, 2 * D_MODEL:])).astype(BF16)


def _conv_in(x, mod, nw, w_bf, is_ctx):
    b, t, _ = x.shape
    tm = _tile(t)
    tok = lambda bb, i: (bb, i, 0)
    const = lambda bb, i: (0, 0)
    return pl.pallas_call(
        _conv_in_kernel,
        grid=(b, t // tm),
        in_specs=[pl.BlockSpec((1, tm, D_MODEL), tok),
                  pl.BlockSpec((1, 1, 3 * D_MODEL), _mod_index(is_ctx, b)),
                  pl.BlockSpec((1, D_MODEL), const),
                  pl.BlockSpec((D_MODEL, 3 * D_MODEL), const)],
        out_specs=[pl.BlockSpec((1, tm, D_MODEL), tok),
                   pl.BlockSpec((1, tm, D_MODEL), tok)],
        out_shape=[jax.ShapeDtypeStruct((b, t, D_MODEL), F32),
                   jax.ShapeDtypeStruct((b, t, D_MODEL), BF16)],
        compiler_params=_params("arbitrary", "arbitrary"),
        name="conv_in_ctx" if is_ctx else "conv_in",
    )(x, mod, nw, w_bf)


def _conv_out_kernel(u_ref, g_ref, x_ref, mod_ref, dw_ref, dwb_ref, lnw_ref, lnb_ref, w_ref,
                     o_ref, upad, conv):
    t = u_ref.shape[1]
    tm = o_ref.shape[1]
    halo = CONV_HALO
    n_cols = D_MODEL // LANES
    i = pl.program_id(1)

    @pl.when(i == 0)
    def _stage():
        for col in range(n_cols):
            upad[col, 0:halo, :] = jnp.zeros((halo, LANES), F32)
            upad[col, halo:halo + t, :] = u_ref[0, :, col * LANES:(col + 1) * LANES]
            upad[col, halo + t:, :] = jnp.zeros((halo, LANES), F32)

    groups = CONV_BLOCK_ROWS // SUBLANES
    row0 = pl.multiple_of(i * tm, SUBLANES)

    def block(n, carry):
        col = n % n_cols
        base = pl.multiple_of((n // n_cols) * CONV_BLOCK_ROWS, CONV_BLOCK_ROWS)
        acc = jnp.zeros((groups, SUBLANES, LANES), F32)
        for k in range(CONV_WIDTH):
            rows = upad[col, pl.ds(row0 + base + (halo - CONV_PAD + k), CONV_BLOCK_ROWS), :]
            acc = acc + rows.reshape(groups, SUBLANES, LANES) * dw_ref[k, col][None]
        conv[col, pl.ds(base, CONV_BLOCK_ROWS), :] = acc.reshape(CONV_BLOCK_ROWS, LANES)
        return carry

    lax.fori_loop(0, (tm // CONV_BLOCK_ROWS) * n_cols, block, 0, unroll=4)

    c = jnp.concatenate([conv[col] for col in range(n_cols)], axis=1) + dwb_ref[...]
    mu = jnp.mean(c, axis=-1, keepdims=True)
    cen = c - mu
    var = jnp.mean(cen * cen, axis=-1, keepdims=True)
    ln = cen * lax.rsqrt(var + EPS) * lnw_ref[...] + lnb_ref[...]
    a = (_silu(ln) * g_ref[0].astype(F32)).astype(BF16)
    gate = mod_ref[0][:, 2 * D_MODEL:]
    o_ref[0] = x_ref[0] + gate * _dot(a, w_ref[...])


def _conv_out(u, g, x, mod, dw_w, dw_b, ln_w, ln_b, w_bf, is_ctx):
    b, t, _ = x.shape
    tm = _tile(t, SEQ_TILE)
    dw_tiles = jnp.broadcast_to(dw_w.reshape(CONV_WIDTH, D_MODEL // LANES, 1, LANES),
                                (CONV_WIDTH, D_MODEL // LANES, SUBLANES, LANES))
    tok = lambda bb, i: (bb, i, 0)
    const = lambda bb, i: (0, 0)
    return pl.pallas_call(
        _conv_out_kernel,
        grid=(b, t // tm),
        in_specs=[pl.BlockSpec((1, t, D_MODEL), lambda bb, i: (bb, 0, 0)),
                  pl.BlockSpec((1, tm, D_MODEL), tok),
                  pl.BlockSpec((1, tm, D_MODEL), tok),
                  pl.BlockSpec((1, 1, 3 * D_MODEL), _mod_index(is_ctx, b)),
                  pl.BlockSpec((CONV_WIDTH, D_MODEL // LANES, SUBLANES, LANES),
                               lambda bb, i: (0, 0, 0, 0)),
                  pl.BlockSpec((1, D_MODEL), const),
                  pl.BlockSpec((1, D_MODEL), const),
                  pl.BlockSpec((1, D_MODEL), const),
                  pl.BlockSpec((D_MODEL, D_MODEL), const)],
        out_specs=pl.BlockSpec((1, tm, D_MODEL), tok),
        out_shape=jax.ShapeDtypeStruct(x.shape, F32),
        scratch_shapes=[pltpu.VMEM((D_MODEL // LANES, t + 2 * CONV_HALO, LANES), F32),
                        pltpu.VMEM((D_MODEL // LANES, tm, LANES), F32)],
        compiler_params=_params("arbitrary", "arbitrary", vmem=VMEM_LIMIT_BIG),
        name="conv_out_ctx" if is_ctx else "conv_out",
    )(u, g, x, mod, dw_tiles, dw_b.reshape(1, -1), ln_w.reshape(1, -1), ln_b.reshape(1, -1), w_bf)


def _rope_tables(n_tokens):
    pos = jnp.arange(n_tokens)
    pos2 = jnp.stack([pos // GRID_W, pos % GRID_W], axis=-1).astype(F32)
    inv_freq = ROPE_THETA ** (-jnp.arange(ROPE_FREQS, dtype=F32) / ROPE_FREQS)
    ang = pos2[:, :, None] * inv_freq
    cos, sin = jnp.cos(ang), jnp.sin(ang)
    cos64 = jnp.concatenate([cos[:, 0], cos[:, 0], cos[:, 1], cos[:, 1]], axis=-1)
    sin64 = jnp.concatenate([-sin[:, 0], sin[:, 0], -sin[:, 1], sin[:, 1]], axis=-1)
    return jnp.tile(cos64, (1, 2)), jnp.tile(sin64, (1, 2))


def _head_mean_matrix():
    idx = np.arange(MXU_DIM) // HEAD_DIM
    return jnp.asarray((idx[:, None] == idx[None, :]).astype(np.float32) / HEAD_DIM, dtype=BF16)


def kernel(x, c, ctx, c_ctx, l0_ada_w, l0_ada_b, l0_norm_w, l0_w_in, l0_q_norm, l0_k_norm, l0_w_out, l1_ada_w, l1_ada_b, l1_norm_w, l1_w_in, l1_w_out, l2_ada_w, l2_ada_b, l2_norm_w, l2_w_in, l2_dw_w, l2_dw_b, l2_ln_w, l2_ln_b, l2_w_out, l3_ada_w, l3_ada_b, l3_norm_w, l3_w_in, l3_q_norm, l3_k_norm, l3_w_out):
    n_batch, seq, _ = x.shape
    n_ctx = ctx.shape[1]

    rows = -(-(n_batch + 1) // SUBLANES) * SUBLANES
    cc = jnp.concatenate([c, c_ctx[None, :], jnp.zeros((rows - n_batch - 1, D_MODEL), F32)], axis=0)

    def mods(ada_w, ada_b):
        return _ada(cc, ada_w, ada_b).reshape(rows, 1, 3 * D_MODEL)

    cos, sin = _rope_tables(seq)
    cos_ctx = jnp.ones((n_ctx, LANES), F32)
    sin_ctx = jnp.zeros((n_ctx, LANES), F32)
    e_mat = _head_mean_matrix()

    def attention_layer(xl, xc, ada_w, ada_b, nw, w_in, qn, kn, w_out, need_ctx, defer_out=False):
        mod = mods(ada_w, ada_b)
        nw2 = nw.reshape(1, -1)
        w_bf = w_in.astype(BF16)
        wo_bf = w_out.astype(BF16)
        qn_t = jnp.tile(qn, N_HEADS).reshape(1, -1)
        kn_t = jnp.tile(kn, N_KV_HEADS).reshape(1, -1)
        q, k, v, g = _attn_in(xl, mod, nw2, w_bf, e_mat, qn_t, kn_t, cos, sin, False)
        ctx_out = _attn_in(xc, mod, nw2, w_bf, e_mat, qn_t, kn_t, cos_ctx, sin_ctx, True,
                           keys_only=not need_ctx)
        kc, vc = (ctx_out[1], ctx_out[2]) if need_ctx else ctx_out
        a = _attention(q, g, [(kc, vc), (k, v)], "attention")
        if defer_out:
            ac = _attention(ctx_out[0], ctx_out[3], [(kc, vc)], "attention_ctx")
            return (a, mod, wo_bf), (ac, mod, wo_bf)
        xl_new = _proj_out(a, xl, mod, wo_bf, False, "attn_out")
        if need_ctx:
            ac = _attention(ctx_out[0], ctx_out[3], [(kc, vc)], "attention_ctx")
            xc = _proj_out(ac, xc, mod, wo_bf, True, "attn_out_ctx")
        return xl_new, xc

    def fourier_layer(xl, xc, ada_w, ada_b, nw, w_in, w_out, prev_l=None, prev_c=None):
        mod = mods(ada_w, ada_b)
        nw2 = nw.reshape(1, -1)
        w_bf = w_in.astype(BF16)
        wo_bf = w_out.astype(BF16)
        cch, sch = _dft_mats(FOURIER_GROUP_DIM, FOURIER_GROUP_DIM ** -0.5)
        cch, sch = jnp.asarray(cch, dtype=BF16), jnp.asarray(sch, dtype=BF16)
        outs = []
        for t_arr, is_ctx, prev in ((xl, False, prev_l), (xc, True, prev_c)):
            t = t_arr.shape[1]
            cp, sp = _dft_mats(t, t ** -0.5)
            dft = jnp.asarray(np.concatenate([cp[:, :t // 2], -sp[:, :t // 2]], axis=1), dtype=BF16)
            rt = min(MXU_DIM, t // 2)
            rev = jnp.asarray(np.eye(rt)[::-1], dtype=BF16)
            res = _fourier_in(t_arr, mod, nw2, w_bf, cch, sch, is_ctx, prev)
            if prev is not None:
                t_arr, res = res[0], res[1:]
            ucs, g = res
            outs.append(_fourier_out(dft, rev, ucs, g, t_arr, mod, wo_bf, is_ctx))
        return outs[0], outs[1]

    def conv_layer(xl, xc, ada_w, ada_b, nw, w_in, dw_w, dw_b, ln_w, ln_b, w_out):
        mod = mods(ada_w, ada_b)
        nw2 = nw.reshape(1, -1)
        w_bf = w_in.astype(BF16)
        wo_bf = w_out.astype(BF16)
        outs = []
        for t_arr, is_ctx in ((xl, False), (xc, True)):
            u, g = _conv_in(t_arr, mod, nw2, w_bf, is_ctx)
            outs.append(_conv_out(u, g, t_arr, mod, dw_w, dw_b, ln_w, ln_b, wo_bf, is_ctx))
        return outs[0], outs[1]

    xl, xc = x, ctx
    prev_l, prev_c = attention_layer(xl, xc, l0_ada_w, l0_ada_b, l0_norm_w, l0_w_in, l0_q_norm,
                                     l0_k_norm, l0_w_out, True, defer_out=True)
    xl, xc = fourier_layer(xl, xc, l1_ada_w, l1_ada_b, l1_norm_w, l1_w_in, l1_w_out, prev_l, prev_c)
    xl, xc = conv_layer(xl, xc, l2_ada_w, l2_ada_b, l2_norm_w, l2_w_in, l2_dw_w, l2_dw_b,
                        l2_ln_w, l2_ln_b, l2_w_out)
    xl, xc = attention_layer(xl, xc, l3_ada_w, l3_ada_b, l3_norm_w, l3_w_in, l3_q_norm, l3_k_norm,
                             l3_w_out, False)
    return xl
```

```python
import functools
import math

import numpy as np
import jax
import jax.numpy as jnp
from jax import lax
from jax.experimental import pallas as pl
from jax.experimental.pallas import tpu as pltpu

D_MODEL = 1024
N_HEADS = 16
N_KV_HEADS = 4
HEAD_DIM = 64
GQA_GROUP = N_HEADS // N_KV_HEADS
Q_W = N_HEADS * HEAD_DIM
KV_W = N_KV_HEADS * HEAD_DIM
GRID_W = 64
ROPE_FREQS = HEAD_DIM // 4
ROPE_THETA = 10000.0
FOURIER_GROUP_DIM = 256
CONV_WIDTH = 31
CONV_PAD = CONV_WIDTH // 2
EPS = 1e-6

LANES = 128
SUBLANES = 8
MXU_DIM = 256
VMEM_LIMIT = 48 * 1024 * 1024
VMEM_LIMIT_BIG = 56 * 1024 * 1024

F32 = jnp.float32
BF16 = jnp.bfloat16

TOKEN_TILE = 1024
SEQ_TILE = 256
Q_TILE = 256
ATTN_UNITS_PER_STEP = 4
CONV_HALO = 2 * SUBLANES
CONV_BLOCK_ROWS = 64


def _params(*sem, vmem=VMEM_LIMIT):
    return pltpu.CompilerParams(dimension_semantics=sem, vmem_limit_bytes=vmem)


def _dot(a, b):
    return jnp.dot(a, b, preferred_element_type=F32)


def _silu(t):
    return t / (1.0 + jnp.exp(-t))


def _sigmoid(t):
    return 1.0 / (1.0 + jnp.exp(-t))


def _split_bf16(t):
    hi = t.astype(BF16)
    lo = (t - hi.astype(F32)).astype(BF16)
    return hi, lo


def _ada_kernel(cc_ref, w_ref, b_ref, o_ref):
    a_hi, a_lo = _split_bf16(_silu(cc_ref[...]))
    w_hi, w_lo = _split_bf16(w_ref[...])
    o_ref[...] = _dot(a_hi, w_hi) + (_dot(a_hi, w_lo) + _dot(a_lo, w_hi)) + b_ref[...]


def _ada(cc, ada_w, ada_b):
    rows = cc.shape[0]
    n = ada_w.shape[1]
    tn = D_MODEL
    return pl.pallas_call(
        _ada_kernel,
        grid=(n // tn,),
        in_specs=[pl.BlockSpec((rows, D_MODEL), lambda j: (0, 0)),
                  pl.BlockSpec((D_MODEL, tn), lambda j: (0, j)),
                  pl.BlockSpec((1, tn), lambda j: (0, j))],
        out_specs=pl.BlockSpec((rows, tn), lambda j: (0, j)),
        out_shape=jax.ShapeDtypeStruct((rows, n), F32),
        compiler_params=_params("arbitrary"),
        name="ada",
    )(cc, ada_w, ada_b.reshape(1, n))


def _modnorm(x, mod, nw):
    r = lax.rsqrt(jnp.mean(x * x, axis=-1, keepdims=True) + EPS)
    shift = mod[:, :D_MODEL]
    scale = mod[:, D_MODEL:2 * D_MODEL]
    return ((x * r) * (nw * (1.0 + scale)) + shift).astype(BF16)


def _mod_index(is_ctx, n_batch):
    if is_ctx:
        return lambda b, i: (n_batch, 0, 0)
    return lambda b, i: (b, 0, 0)


def _tile(t, rows=TOKEN_TILE):
    return min(rows, t)


def _head_rms(t, e_ref, w):
    outs = []
    for c in range(t.shape[1] // MXU_DIM):
        tc = t[:, c * MXU_DIM:(c + 1) * MXU_DIM]
        ms = _dot((tc * tc).astype(BF16), e_ref[...])
        outs.append(tc * lax.rsqrt(ms + EPS))
    return jnp.concatenate(outs, axis=1) * w


def _rope(t, cos, sin):
    lane = lax.broadcasted_iota(jnp.int32, (t.shape[0], LANES), 1)
    low_half = (lane & ROPE_FREQS) == 0
    outs = []
    for c in range(t.shape[1] // LANES):
        tc = t[:, c * LANES:(c + 1) * LANES]
        partner = jnp.where(low_half,
                            pltpu.roll(tc, LANES - ROPE_FREQS, 1),
                            pltpu.roll(tc, ROPE_FREQS, 1))
        outs.append(tc * cos + partner * sin)
    return jnp.concatenate(outs, axis=1)


def _attn_in_kernel(x_ref, mod_ref, nw_ref, w_ref, e_ref, qn_ref, kn_ref, cos_ref, sin_ref,
                    *out_refs, keys_only):
    h = _modnorm(x_ref[0], mod_ref[0], nw_ref[...])
    cos = cos_ref[...]
    sin = sin_ref[...]
    if keys_only:
        k_ref, v_ref = out_refs
        k_raw = _dot(h, w_ref[:, Q_W:Q_W + KV_W])
    else:
        q_ref, k_ref, v_ref, g_ref = out_refs
        q_raw = _dot(h, w_ref[:, :Q_W])
        k_raw = _dot(h, w_ref[:, Q_W:Q_W + KV_W])
        q = _head_rms(q_raw, e_ref, qn_ref[...])
        q_ref[0] = (_rope(q, cos, sin) * (HEAD_DIM ** -0.5)).astype(BF16)
        g_ref[0] = _silu(_dot(h, w_ref[:, Q_W + 2 * KV_W:])).astype(BF16)
    v_raw = _dot(h, w_ref[:, Q_W + KV_W:Q_W + 2 * KV_W])
    k = _head_rms(k_raw, e_ref, kn_ref[...])
    k_ref[0] = _rope(k, cos, sin).astype(BF16)
    v_ref[0] = v_raw.astype(BF16)


def _attn_in(x, mod, nw, w_bf, e_mat, qn, kn, cos, sin, is_ctx, keys_only=False):
    b, t, _ = x.shape
    tm = _tile(t)
    n_in = w_bf.shape[1]
    tok = lambda bb, i: (bb, i, 0)
    const = lambda bb, i: (0, 0)
    widths = (KV_W, KV_W) if keys_only else (Q_W, KV_W, KV_W, Q_W)
    return pl.pallas_call(
        functools.partial(_attn_in_kernel, keys_only=keys_only),
        grid=(b, t // tm),
        in_specs=[pl.BlockSpec((1, tm, D_MODEL), tok),
                  pl.BlockSpec((1, 1, 3 * D_MODEL), _mod_index(is_ctx, b)),
                  pl.BlockSpec((1, D_MODEL), const),
                  pl.BlockSpec((D_MODEL, n_in), const),
                  pl.BlockSpec((MXU_DIM, MXU_DIM), const),
                  pl.BlockSpec((1, Q_W), const),
                  pl.BlockSpec((1, KV_W), const),
                  pl.BlockSpec((tm, LANES), lambda bb, i: (i, 0)),
                  pl.BlockSpec((tm, LANES), lambda bb, i: (i, 0))],
        out_specs=[pl.BlockSpec((1, tm, w), tok) for w in widths],
        out_shape=[jax.ShapeDtypeStruct((b, t, w), BF16) for w in widths],
        compiler_params=_params("arbitrary", "arbitrary"),
        name="attn_in_ctx" if is_ctx else "attn_in",
    )(x, mod, nw, w_bf, e_mat, qn, kn, cos, sin)


V_ROWS = HEAD_DIM + 16

def _attn_kernel(*refs, lens, n_qtiles, n_units, tq, per_step):
    n_src = len(lens)
    q_ref, g_ref = refs[0], refs[1]
    k_refs = refs[2:2 + n_src]
    v_refs = refs[2 + n_src:2 + 2 * n_src]
    o_ref = refs[2 + 2 * n_src]
    k_lo, k_hi, v_t, s_buf, m_buf, o_buf = refs[3 + 2 * n_src:]
    m = pl.program_id(0)
    n_keys = sum(lens)
    half = LANES // 2
    units_per_batch = N_KV_HEADS * n_qtiles
    first = jnp.minimum(per_step * m, n_units - per_step)
    new_batch = (first % units_per_batch == 0) & (per_step * m < n_units)

    @pl.when(m == 0)
    def _fill():
        s_buf[1] = jnp.zeros(s_buf.shape[1:], F32)
        m_buf[1] = jnp.zeros(m_buf.shape[1:], F32)
        o_buf[...] = jnp.ones(o_buf.shape, F32)

    @pl.when(new_batch)
    def _stage():
        zeros = jnp.zeros((n_keys, half), BF16)
        row = lax.broadcasted_iota(jnp.int32, (V_ROWS - HEAD_DIM, n_keys), 0)
        ones_row = jnp.where(row == 0, 1.0, 0.0).astype(BF16)
        v_slot = (first // units_per_batch) % 2
        vts = [v_refs[s][0].astype(F32).T.astype(BF16) for s in range(n_src)]
        for gg in range(N_KV_HEADS):
            k_lo[gg, :, half:] = zeros
            k_hi[gg, :, :half] = zeros
            v_t[v_slot, gg, HEAD_DIM:, :] = ones_row
            off = 0
            for s in range(n_src):
                kk = k_refs[s][0, :, gg * HEAD_DIM:(gg + 1) * HEAD_DIM]
                k_lo[gg, off:off + lens[s], :half] = kk
                k_hi[gg, off:off + lens[s], half:] = kk
                v_t[v_slot, gg, :HEAD_DIM, off:off + lens[s]] = vts[s][gg * HEAD_DIM:(gg + 1) * HEAD_DIM, :]
                off += lens[s]

    def col_max(t):
        while t.shape[0] % (2 * SUBLANES) == 0:
            n = t.shape[0] // 2
            t = jnp.maximum(t[:n, :], t[n:, :])
        return jnp.max(t, axis=0, keepdims=True)

    def sub_step(j):
        write_slot, read_slot = j % 2, (j - 1) % 2
        acc_w, acc_r = (j - 1) % per_step, j
        rows = slice(j * tq, (j + 1) * tq)
        grp = (first // n_qtiles) % N_KV_HEADS
        weighed = jnp.clip(per_step * m + j - 1, 0, n_units - 1)
        vt_g = v_t[(weighed // units_per_batch) % 2, (weighed // n_qtiles) % N_KV_HEADS]

        def score(h):
            q2 = q_ref[0, rows, (h // 2) * LANES:(h // 2 + 1) * LANES]
            k_st = k_hi if h % 2 else k_lo
            s_t = lax.dot_general(k_st[grp], q2, (((1,), (1,)), ((), ())),
                                  preferred_element_type=F32)
            s_buf[write_slot, h] = s_t
            m_buf[write_slot, h] = col_max(s_t)

        def weigh(h):
            p_t = jnp.exp(s_buf[read_slot, h] - m_buf[read_slot, h]).astype(BF16)
            o_buf[acc_w, h] = _dot(vt_g, p_t)

        o_rows = [o_buf[acc_r, h, :HEAD_DIM, :] / o_buf[acc_r, h, HEAD_DIM:HEAD_DIM + 1, :]
                  for h in range(GQA_GROUP)]
        o = jnp.concatenate(o_rows, axis=0).T
        o_ref[0, rows, :] = (o * g_ref[0, rows, :].astype(F32)).astype(BF16)

        for pair in range(GQA_GROUP // 2):
            score(2 * pair)
            weigh(2 * pair)
            weigh(2 * pair + 1)
            score(2 * pair + 1)

    for j in range(per_step):
        sub_step(j)


def _attention(q, g, kvs, name):
    b, t, _ = q.shape
    per_step = max(2, min(ATTN_UNITS_PER_STEP, t // Q_TILE))
    tq = t // max(per_step, t // Q_TILE)
    n_qtiles = t // tq
    n_units = b * N_KV_HEADS * n_qtiles
    n_steps = n_units // per_step
    lens = tuple(int(k.shape[1]) for k, _ in kvs)
    n_keys = sum(lens)

    def step_block(s):
        u = per_step * s
        return (u // (N_KV_HEADS * n_qtiles), (u % n_qtiles) // per_step, (u // n_qtiles) % N_KV_HEADS)

    def step_batch(s):
        return ((per_step * s) // (N_KV_HEADS * n_qtiles), 0, 0)

    scored = lambda m: jnp.minimum(m, n_steps - 1)
    stored = lambda m: jnp.maximum(m - 1, 0)
    width = GQA_GROUP * HEAD_DIM
    in_specs = [pl.BlockSpec((1, per_step * tq, width), lambda m: step_block(scored(m))),
                pl.BlockSpec((1, per_step * tq, width), lambda m: step_block(stored(m)))]
    in_specs += [pl.BlockSpec((1, ln, KV_W), lambda m: step_batch(scored(m))) for ln in lens] * 2
    args = [q, g] + [k for k, _ in kvs] + [v for _, v in kvs]
    return pl.pallas_call(
        functools.partial(_attn_kernel, lens=lens, n_qtiles=n_qtiles, n_units=n_units, tq=tq,
                          per_step=per_step),
        grid=(n_steps + 1,),
        in_specs=in_specs,
        out_specs=pl.BlockSpec((1, per_step * tq, width), lambda m: step_block(stored(m))),
        out_shape=jax.ShapeDtypeStruct((b, t, Q_W), BF16),
        scratch_shapes=[pltpu.VMEM((N_KV_HEADS, n_keys, LANES), BF16),
                        pltpu.VMEM((N_KV_HEADS, n_keys, LANES), BF16),
                        pltpu.VMEM((2, N_KV_HEADS, V_ROWS, n_keys), BF16),
                        pltpu.VMEM((2, GQA_GROUP, n_keys, tq), F32),
                        pltpu.VMEM((2, GQA_GROUP, 1, tq), F32),
                        pltpu.VMEM((per_step, GQA_GROUP, V_ROWS, tq), F32)],
        compiler_params=_params("arbitrary", vmem=VMEM_LIMIT_BIG),
        name=name,
    )(*args)


def _proj_out_kernel(a_ref, x_ref, mod_ref, w_ref, o_ref):
    gate = mod_ref[0][:, 2 * D_MODEL:]
    o_ref[0] = x_ref[0] + gate * _dot(a_ref[0], w_ref[...])


def _proj_out(a, x, mod, w_bf, is_ctx, name):
    b, t, _ = x.shape
    tm = _tile(t)
    tok = lambda bb, i: (bb, i, 0)
    return pl.pallas_call(
        _proj_out_kernel,
        grid=(b, t // tm),
        in_specs=[pl.BlockSpec((1, tm, D_MODEL), tok),
                  pl.BlockSpec((1, tm, D_MODEL), tok),
                  pl.BlockSpec((1, 1, 3 * D_MODEL), _mod_index(is_ctx, b)),
                  pl.BlockSpec((D_MODEL, D_MODEL), lambda bb, i: (0, 0))],
        out_specs=pl.BlockSpec((1, tm, D_MODEL), tok),
        out_shape=jax.ShapeDtypeStruct(x.shape, F32),
        compiler_params=_params("arbitrary", "arbitrary"),
        name=name,
    )(a, x, mod, w_bf)


def _dft_mats(n, scale):
    idx = np.arange(n, dtype=np.int64)
    ang = 2.0 * np.pi * ((idx[:, None] * idx[None, :]) % n).astype(np.float64) / n
    return np.cos(ang) * scale, np.sin(ang) * scale


def _fourier_in_kernel(*refs, after_attention):
    if after_attention:
        a_ref, pmod_ref, pw_ref = refs[:3]
        x_ref, mod_ref, nw_ref, w_ref, cc_ref, sc_ref, xo_ref, uc_ref, g_ref = refs[3:]
        x = x_ref[0] + pmod_ref[0][:, 2 * D_MODEL:] * _dot(a_ref[0], pw_ref[...])
        xo_ref[0] = x
    else:
        x_ref, mod_ref, nw_ref, w_ref, cc_ref, sc_ref, uc_ref, g_ref = refs
        x = x_ref[0]
    h = _modnorm(x, mod_ref[0], nw_ref[...])
    u = _dot(h, w_ref[:, :D_MODEL]).astype(BF16)
    g_ref[0] = _silu(_dot(h, w_ref[:, D_MODEL:])).astype(BF16)
    ucs, uss = [], []
    for c in range(D_MODEL // FOURIER_GROUP_DIM):
        ug = u[:, c * FOURIER_GROUP_DIM:(c + 1) * FOURIER_GROUP_DIM]
        ucs.append(_dot(ug, cc_ref[...]))
        uss.append(_dot(ug, sc_ref[...]))
    uc_ref[0, 0] = jnp.concatenate(ucs, axis=1).astype(BF16)
    uc_ref[0, 1] = jnp.concatenate(uss, axis=1).astype(BF16)


def _fourier_in(x, mod, nw, w_bf, cmat, smat, is_ctx, prev=None):
    b, t, _ = x.shape
    tm = _tile(t)
    tok = lambda bb, i: (bb, i, 0)
    const = lambda bb, i: (0, 0)
    mod_spec = pl.BlockSpec((1, 1, 3 * D_MODEL), _mod_index(is_ctx, b))
    in_specs = [pl.BlockSpec((1, tm, D_MODEL), tok),
                mod_spec,
                pl.BlockSpec((1, D_MODEL), const),
                pl.BlockSpec((D_MODEL, 2 * D_MODEL), const),
                pl.BlockSpec((FOURIER_GROUP_DIM, FOURIER_GROUP_DIM), const),
                pl.BlockSpec((FOURIER_GROUP_DIM, FOURIER_GROUP_DIM), const)]
    out_specs = [pl.BlockSpec((1, 2, tm, D_MODEL), lambda bb, i: (bb, 0, i, 0)),
                 pl.BlockSpec((1, tm, D_MODEL), tok)]
    out_shape = [jax.ShapeDtypeStruct((b, 2, t, D_MODEL), BF16),
                 jax.ShapeDtypeStruct((b, t, D_MODEL), BF16)]
    args = (x, mod, nw, w_bf, cmat, smat)
    if prev is not None:
        in_specs = [pl.BlockSpec((1, tm, D_MODEL), tok), mod_spec,
                    pl.BlockSpec((D_MODEL, D_MODEL), const)] + in_specs
        out_specs = [pl.BlockSpec((1, tm, D_MODEL), tok)] + out_specs
        out_shape = [jax.ShapeDtypeStruct(x.shape, F32)] + out_shape
        args = tuple(prev) + args
    return pl.pallas_call(
        functools.partial(_fourier_in_kernel, after_attention=prev is not None),
        grid=(b, t // tm),
        in_specs=in_specs,
        out_specs=out_specs,
        out_shape=out_shape,
        compiler_params=_params("arbitrary", "arbitrary"),
        name="fourier_in_ctx" if is_ctx else "fourier_in",
    )(*args)


def _fourier_out_kernel(dft_ref, rev_ref, uc_ref, g_ref, x_ref, mod_ref, w_ref, o_ref, folded):
    n_pos = uc_ref.shape[1] // 2
    half = n_pos // 2
    tm = o_ref.shape[1]
    rt = rev_ref.shape[0]

    @pl.when(pl.program_id(1) == 0)
    def _fold():
        first_row = lax.broadcasted_iota(jnp.int32, (half, D_MODEL), 0) == 0
        for part, sign in ((0, 1.0), (1, -1.0)):
            base = part * n_pos
            tiles = [_dot(rev_ref[...], uc_ref[0, pl.ds(base + n_pos - (j + 1) * rt, rt), :])
                     for j in range(half // rt)]
            rev = jnp.concatenate(tiles, axis=0)
            mirror = jnp.where(first_row, 0.0, pltpu.roll(rev, 1, 0))
            folded[pl.ds(part * half, half), :] = (
                uc_ref[0, pl.ds(base, half), :].astype(F32) + sign * mirror).astype(BF16)

    mid = uc_ref[0, half:half + 2 * SUBLANES, :].astype(F32)[0:1, :] * (n_pos ** -0.5)
    gate = mod_ref[0][:, 2 * D_MODEL:]
    chunk = min(tm, MXU_DIM)
    spans = [slice(r, r + chunk) for r in range(0, tm, chunk)]
    odd = (lax.broadcasted_iota(jnp.int32, (chunk, D_MODEL), 0) & 1) == 1
    row0 = pl.multiple_of(pl.program_id(1) * tm, chunk)
    fs = [_dot(dft_ref[pl.ds(row0 + rows.start, chunk), :], folded[...]) for rows in spans]
    for rows, f in zip(spans, fs):
        f = f + jnp.where(odd, -mid, mid)
        a = (f * g_ref[0, rows, :].astype(F32)).astype(BF16)
        o_ref[0, rows, :] = x_ref[0, rows, :] + gate * _dot(a, w_ref[...])


def _fourier_out(dft, rev, ucs, g, x, mod, w_bf, is_ctx):
    b, t, _ = x.shape
    tm = _tile(t, 2 * SEQ_TILE)
    tok = lambda bb, i: (bb, i, 0)
    return pl.pallas_call(
        _fourier_out_kernel,
        grid=(b, t // tm),
        in_specs=[pl.BlockSpec((t, t), lambda bb, i: (0, 0), pipeline_mode=pl.Buffered(1)),
                  pl.BlockSpec(rev.shape, lambda bb, i: (0, 0)),
                  pl.BlockSpec((1, 2 * t, D_MODEL), lambda bb, i: (bb, 0, 0)),
                  pl.BlockSpec((1, tm, D_MODEL), tok),
                  pl.BlockSpec((1, tm, D_MODEL), tok),
                  pl.BlockSpec((1, 1, 3 * D_MODEL), _mod_index(is_ctx, b)),
                  pl.BlockSpec((D_MODEL, D_MODEL), lambda bb, i: (0, 0))],
        out_specs=pl.BlockSpec((1, tm, D_MODEL), tok),
        out_shape=jax.ShapeDtypeStruct(x.shape, F32),
        scratch_shapes=[pltpu.VMEM((t, D_MODEL), BF16)],
        compiler_params=_params("arbitrary", "arbitrary", vmem=VMEM_LIMIT_BIG),
        name="fourier_out_ctx" if is_ctx else "fourier_out",
    )(dft, rev, ucs.reshape(b, 2 * t, D_MODEL), g, x, mod, w_bf)


def _conv_in_kernel(x_ref, mod_ref, nw_ref, w_ref, u_ref, g_ref):
    h = _modnorm(x_ref[0], mod_ref[0], nw_ref[...])
    a = _dot(h, w_ref[:, :D_MODEL])
    gl = _dot(h, w_ref[:, D_MODEL:2 * D_MODEL])
    u_ref[0] = a * _sigmoid(gl)
    g_ref[0] = _silu(_dot(h, w_ref[:, 2 * D_MODEL:])).astype(BF16)


def _conv_in(x, mod, nw, w_bf, is_ctx):
    b, t, _ = x.shape
    tm = _tile(t)
    tok = lambda bb, i: (bb, i, 0)
    const = lambda bb, i: (0, 0)
    return pl.pallas_call(
        _conv_in_kernel,
        grid=(b, t // tm),
        in_specs=[pl.BlockSpec((1, tm, D_MODEL), tok),
                  pl.BlockSpec((1, 1, 3 * D_MODEL), _mod_index(is_ctx, b)),
                  pl.BlockSpec((1, D_MODEL), const),
                  pl.BlockSpec((D_MODEL, 3 * D_MODEL), const)],
        out_specs=[pl.BlockSpec((1, tm, D_MODEL), tok),
                   pl.BlockSpec((1, tm, D_MODEL), tok)],
        out_shape=[jax.ShapeDtypeStruct((b, t, D_MODEL), F32),
                   jax.ShapeDtypeStruct((b, t, D_MODEL), BF16)],
        compiler_params=_params("arbitrary", "arbitrary"),
        name="conv_in_ctx" if is_ctx else "conv_in",
    )(x, mod, nw, w_bf)


def _conv_out_kernel(u_ref, g_ref, x_ref, mod_ref, dw_ref, dwb_ref, lnw_ref, lnb_ref, w_ref,
                     o_ref, upad, conv):
    t = u_ref.shape[1]
    tm = o_ref.shape[1]
    halo = CONV_HALO
    n_cols = D_MODEL // LANES
    i = pl.program_id(1)

    @pl.when(i == 0)
    def _stage():
        for col in range(n_cols):
            upad[col, 0:halo, :] = jnp.zeros((halo, LANES), F32)
            upad[col, halo:halo + t, :] = u_ref[0, :, col * LANES:(col + 1) * LANES]
            upad[col, halo + t:, :] = jnp.zeros((halo, LANES), F32)

    groups = CONV_BLOCK_ROWS // SUBLANES
    row0 = pl.multiple_of(i * tm, SUBLANES)

    def block(n, carry):
        col = n % n_cols
        base = pl.multiple_of((n // n_cols) * CONV_BLOCK_ROWS, CONV_BLOCK_ROWS)
        acc = jnp.zeros((groups, SUBLANES, LANES), F32)
        for k in range(CONV_WIDTH):
            rows = upad[col, pl.ds(row0 + base + (halo - CONV_PAD + k), CONV_BLOCK_ROWS), :]
            acc = acc + rows.reshape(groups, SUBLANES, LANES) * dw_ref[k, col][None]
        conv[col, pl.ds(base, CONV_BLOCK_ROWS), :] = acc.reshape(CONV_BLOCK_ROWS, LANES)
        return carry

    lax.fori_loop(0, (tm // CONV_BLOCK_ROWS) * n_cols, block, 0, unroll=4)

    c = jnp.concatenate([conv[col] for col in range(n_cols)], axis=1) + dwb_ref[...]
    mu = jnp.mean(c, axis=-1, keepdims=True)
    cen = c - mu
    var = jnp.mean(cen * cen, axis=-1, keepdims=True)
    ln = cen * lax.rsqrt(var + EPS) * lnw_ref[...] + lnb_ref[...]
    a = (_silu(ln) * g_ref[0].astype(F32)).astype(BF16)
    gate = mod_ref[0][:, 2 * D_MODEL:]
    o_ref[0] = x_ref[0] + gate * _dot(a, w_ref[...])


def _conv_out(u, g, x, mod, dw_w, dw_b, ln_w, ln_b, w_bf, is_ctx):
    b, t, _ = x.shape
    tm = _tile(t, SEQ_TILE)
    dw_tiles = jnp.broadcast_to(dw_w.reshape(CONV_WIDTH, D_MODEL // LANES, 1, LANES),
                                (CONV_WIDTH, D_MODEL // LANES, SUBLANES, LANES))
    tok = lambda bb, i: (bb, i, 0)
    const = lambda bb, i: (0, 0)
    return pl.pallas_call(
        _conv_out_kernel,
        grid=(b, t // tm),
        in_specs=[pl.BlockSpec((1, t, D_MODEL), lambda bb, i: (bb, 0, 0)),
                  pl.BlockSpec((1, tm, D_MODEL), tok),
                  pl.BlockSpec((1, tm, D_MODEL), tok),
                  pl.BlockSpec((1, 1, 3 * D_MODEL), _mod_index(is_ctx, b)),
                  pl.BlockSpec((CONV_WIDTH, D_MODEL // LANES, SUBLANES, LANES),
                               lambda bb, i: (0, 0, 0, 0)),
                  pl.BlockSpec((1, D_MODEL), const),
                  pl.BlockSpec((1, D_MODEL), const),
                  pl.BlockSpec((1, D_MODEL), const),
                  pl.BlockSpec((D_MODEL, D_MODEL), const)],
        out_specs=pl.BlockSpec((1, tm, D_MODEL), tok),
        out_shape=jax.ShapeDtypeStruct(x.shape, F32),
        scratch_shapes=[pltpu.VMEM((D_MODEL // LANES, t + 2 * CONV_HALO, LANES), F32),
                        pltpu.VMEM((D_MODEL // LANES, tm, LANES), F32)],
        compiler_params=_params("arbitrary", "arbitrary", vmem=VMEM_LIMIT_BIG),
        name="conv_out_ctx" if is_ctx else "conv_out",
    )(u, g, x, mod, dw_tiles, dw_b.reshape(1, -1), ln_w.reshape(1, -1), ln_b.reshape(1, -1), w_bf)


def _rope_tables(n_tokens):
    pos = jnp.arange(n_tokens)
    pos2 = jnp.stack([pos // GRID_W, pos % GRID_W], axis=-1).astype(F32)
    inv_freq = ROPE_THETA ** (-jnp.arange(ROPE_FREQS, dtype=F32) / ROPE_FREQS)
    ang = pos2[:, :, None] * inv_freq
    cos, sin = jnp.cos(ang), jnp.sin(ang)
    cos64 = jnp.concatenate([cos[:, 0], cos[:, 0], cos[:, 1], cos[:, 1]], axis=-1)
    sin64 = jnp.concatenate([-sin[:, 0], sin[:, 0], -sin[:, 1], sin[:, 1]], axis=-1)
    return jnp.tile(cos64, (1, 2)), jnp.tile(sin64, (1, 2))


def _head_mean_matrix():
    idx = np.arange(MXU_DIM) // HEAD_DIM
    return jnp.asarray((idx[:, None] == idx[None, :]).astype(np.float32) / HEAD_DIM, dtype=BF16)


def kernel(x, c, ctx, c_ctx, l0_ada_w, l0_ada_b, l0_norm_w, l0_w_in, l0_q_norm, l0_k_norm, l0_w_out, l1_ada_w, l1_ada_b, l1_norm_w, l1_w_in, l1_w_out, l2_ada_w, l2_ada_b, l2_norm_w, l2_w_in, l2_dw_w, l2_dw_b, l2_ln_w, l2_ln_b, l2_w_out, l3_ada_w, l3_ada_b, l3_norm_w, l3_w_in, l3_q_norm, l3_k_norm, l3_w_out):
    n_batch, seq, _ = x.shape
    n_ctx = ctx.shape[1]

    rows = -(-(n_batch + 1) // SUBLANES) * SUBLANES
    cc = jnp.concatenate([c, c_ctx[None, :], jnp.zeros((rows - n_batch - 1, D_MODEL), F32)], axis=0)

    def mods(ada_w, ada_b):
        return _ada(cc, ada_w, ada_b).reshape(rows, 1, 3 * D_MODEL)

    cos, sin = _rope_tables(seq)
    cos_ctx = jnp.ones((n_ctx, LANES), F32)
    sin_ctx = jnp.zeros((n_ctx, LANES), F32)
    e_mat = _head_mean_matrix()

    def attention_layer(xl, xc, ada_w, ada_b, nw, w_in, qn, kn, w_out, need_ctx, defer_out=False):
        mod = mods(ada_w, ada_b)
        nw2 = nw.reshape(1, -1)
        w_bf = w_in.astype(BF16)
        wo_bf = w_out.astype(BF16)
        qn_t = jnp.tile(qn, N_HEADS).reshape(1, -1)
        kn_t = jnp.tile(kn, N_KV_HEADS).reshape(1, -1)
        q, k, v, g = _attn_in(xl, mod, nw2, w_bf, e_mat, qn_t, kn_t, cos, sin, False)
        ctx_out = _attn_in(xc, mod, nw2, w_bf, e_mat, qn_t, kn_t, cos_ctx, sin_ctx, True,
                           keys_only=not need_ctx)
        kc, vc = (ctx_out[1], ctx_out[2]) if need_ctx else ctx_out
        a = _attention(q, g, [(kc, vc), (k, v)], "attention")
        if defer_out:
            ac = _attention(ctx_out[0], ctx_out[3], [(kc, vc)], "attention_ctx")
            return (a, mod, wo_bf), (ac, mod, wo_bf)
        xl_new = _proj_out(a, xl, mod, wo_bf, False, "attn_out")
        if need_ctx:
            ac = _attention(ctx_out[0], ctx_out[3], [(kc, vc)], "attention_ctx")
            xc = _proj_out(ac, xc, mod, wo_bf, True, "attn_out_ctx")
        return xl_new, xc

    def fourier_layer(xl, xc, ada_w, ada_b, nw, w_in, w_out, prev_l=None, prev_c=None):
        mod = mods(ada_w, ada_b)
        nw2 = nw.reshape(1, -1)
        w_bf = w_in.astype(BF16)
        wo_bf = w_out.astype(BF16)
        cch, sch = _dft_mats(FOURIER_GROUP_DIM, FOURIER_GROUP_DIM ** -0.5)
        cch, sch = jnp.asarray(cch, dtype=BF16), jnp.asarray(sch, dtype=BF16)
        outs = []
        for t_arr, is_ctx, prev in ((xl, False, prev_l), (xc, True, prev_c)):
            t = t_arr.shape[1]
            cp, sp = _dft_mats(t, t ** -0.5)
            dft = jnp.asarray(np.concatenate([cp[:, :t // 2], -sp[:, :t // 2]], axis=1), dtype=BF16)
            rt = min(MXU_DIM, t // 2)
            rev = jnp.asarray(np.eye(rt)[::-1], dtype=BF16)
            res = _fourier_in(t_arr, mod, nw2, w_bf, cch, sch, is_ctx, prev)
            if prev is not None:
                t_arr, res = res[0], res[1:]
            ucs, g = res
            outs.append(_fourier_out(dft, rev, ucs, g, t_arr, mod, wo_bf, is_ctx))
        return outs[0], outs[1]

    def conv_layer(xl, xc, ada_w, ada_b, nw, w_in, dw_w, dw_b, ln_w, ln_b, w_out):
        mod = mods(ada_w, ada_b)
        nw2 = nw.reshape(1, -1)
        w_bf = w_in.astype(BF16)
        wo_bf = w_out.astype(BF16)
        outs = []
        for t_arr, is_ctx in ((xl, False), (xc, True)):
            u, g = _conv_in(t_arr, mod, nw2, w_bf, is_ctx)
            outs.append(_conv_out(u, g, t_arr, mod, dw_w, dw_b, ln_w, ln_b, wo_bf, is_ctx))
        return outs[0], outs[1]

    xl, xc = x, ctx
    prev_l, prev_c = attention_layer(xl, xc, l0_ada_w, l0_ada_b, l0_norm_w, l0_w_in, l0_q_norm,
                                     l0_k_norm, l0_w_out, True, defer_out=True)
    xl, xc = fourier_layer(xl, xc, l1_ada_w, l1_ada_b, l1_norm_w, l1_w_in, l1_w_out, prev_l, prev_c)
    xl, xc = conv_layer(xl, xc, l2_ada_w, l2_ada_b, l2_norm_w, l2_w_in, l2_dw_w, l2_dw_b,
                        l2_ln_w, l2_ln_b, l2_w_out)
    xl, xc = attention_layer(xl, xc, l3_ada_w, l3_ada_b, l3_norm_w, l3_w_in, l3_q_norm, l3_k_norm,
                             l3_w_out, False)
    return xl
```

```python
import functools

import numpy as np
import jax
import jax.numpy as jnp
from jax import lax
from jax.experimental import pallas as pl
from jax.experimental.pallas import tpu as pltpu

D_MODEL = 1024
N_HEADS = 16
N_KV_HEADS = 4
HEAD_DIM = 64
GQA_GROUP = N_HEADS // N_KV_HEADS
Q_W = N_HEADS * HEAD_DIM
KV_W = N_KV_HEADS * HEAD_DIM
GRID_W = 64
ROPE_FREQS = HEAD_DIM // 4
ROPE_THETA = 10000.0
FOURIER_GROUP_DIM = 256
CONV_WIDTH = 31
CONV_PAD = CONV_WIDTH // 2
EPS = 1e-6

LANES = 128
SUBLANES = 8
MXU_DIM = 256
VMEM_LIMIT = 48 * 1024 * 1024
VMEM_LIMIT_BIG = 56 * 1024 * 1024

F32 = jnp.float32
BF16 = jnp.bfloat16

TOKEN_TILE = 1024
SEQ_TILE = 256
Q_TILE = 256
ATTN_UNITS_PER_STEP = 4
CONV_HALO = 2 * SUBLANES
CONV_BLOCK_ROWS = 64


def _params(*sem, vmem=VMEM_LIMIT):
    return pltpu.CompilerParams(dimension_semantics=sem, vmem_limit_bytes=vmem)


def _dot(a, b):
    return jnp.dot(a, b, preferred_element_type=F32)


def _silu(t):
    return t / (1.0 + jnp.exp(-t))


def _sigmoid(t):
    return 1.0 / (1.0 + jnp.exp(-t))


def _split_bf16(t):
    hi = t.astype(BF16)
    lo = (t - hi.astype(F32)).astype(BF16)
    return hi, lo


def _ada_kernel(cc_ref, w_ref, b_ref, o_ref):
    a_hi, a_lo = _split_bf16(_silu(cc_ref[...]))
    w_hi, w_lo = _split_bf16(w_ref[...])
    o_ref[...] = _dot(a_hi, w_hi) + (_dot(a_hi, w_lo) + _dot(a_lo, w_hi)) + b_ref[...]


def _ada(cc, ada_w, ada_b):
    rows = cc.shape[0]
    n = ada_w.shape[1]
    tn = D_MODEL
    return pl.pallas_call(
        _ada_kernel,
        grid=(n // tn,),
        in_specs=[pl.BlockSpec((rows, D_MODEL), lambda j: (0, 0)),
                  pl.BlockSpec((D_MODEL, tn), lambda j: (0, j)),
                  pl.BlockSpec((1, tn), lambda j: (0, j))],
        out_specs=pl.BlockSpec((rows, tn), lambda j: (0, j)),
        out_shape=jax.ShapeDtypeStruct((rows, n), F32),
        compiler_params=_params("arbitrary"),
        name="ada",
    )(cc, ada_w, ada_b.reshape(1, n))


def _modnorm(x, mod, nw):
    r = lax.rsqrt(jnp.mean(x * x, axis=-1, keepdims=True) + EPS)
    shift = mod[:, :D_MODEL]
    scale = mod[:, D_MODEL:2 * D_MODEL]
    return ((x * r) * (nw * (1.0 + scale)) + shift).astype(BF16)


def _mod_index(is_ctx, n_batch):
    if is_ctx:
        return lambda b, i: (n_batch, 0, 0)
    return lambda b, i: (b, 0, 0)


def _tile(t, rows=TOKEN_TILE):
    return min(rows, t)


def _head_rms(t, e_ref, w):
    outs = []
    for c in range(t.shape[1] // MXU_DIM):
        tc = t[:, c * MXU_DIM:(c + 1) * MXU_DIM]
        ms = _dot((tc * tc).astype(BF16), e_ref[...])
        outs.append(tc * lax.rsqrt(ms + EPS))
    return jnp.concatenate(outs, axis=1) * w


def _rope(t, cos, sin):
    lane = lax.broadcasted_iota(jnp.int32, (t.shape[0], LANES), 1)
    low_half = (lane & ROPE_FREQS) == 0
    outs = []
    for c in range(t.shape[1] // LANES):
        tc = t[:, c * LANES:(c + 1) * LANES]
        partner = jnp.where(low_half,
                            pltpu.roll(tc, LANES - ROPE_FREQS, 1),
                            pltpu.roll(tc, ROPE_FREQS, 1))
        outs.append(tc * cos + partner * sin)
    return jnp.concatenate(outs, axis=1)


def _attn_in_kernel(x_ref, mod_ref, nw_ref, w_ref, e_ref, qn_ref, kn_ref, cos_ref, sin_ref,
                    *out_refs, keys_only):
    h = _modnorm(x_ref[0], mod_ref[0], nw_ref[...])
    cos = cos_ref[...]
    sin = sin_ref[...]
    if keys_only:
        k_ref, v_ref = out_refs
        k_raw = _dot(h, w_ref[:, Q_W:Q_W + KV_W])
    else:
        q_ref, k_ref, v_ref, g_ref = out_refs
        q_raw = _dot(h, w_ref[:, :Q_W])
        k_raw = _dot(h, w_ref[:, Q_W:Q_W + KV_W])
        q = _head_rms(q_raw, e_ref, qn_ref[...])
        q_ref[0] = (_rope(q, cos, sin) * (HEAD_DIM ** -0.5)).astype(BF16)
        g_ref[0] = _silu(_dot(h, w_ref[:, Q_W + 2 * KV_W:])).astype(BF16)
    v_raw = _dot(h, w_ref[:, Q_W + KV_W:Q_W + 2 * KV_W])
    k = _head_rms(k_raw, e_ref, kn_ref[...])
    k_ref[0] = _rope(k, cos, sin).astype(BF16)
    v_ref[0] = v_raw.astype(BF16)


def _attn_in(x, mod, nw, w_bf, e_mat, qn, kn, cos, sin, is_ctx, keys_only=False):
    b, t, _ = x.shape
    tm = _tile(t)
    n_in = w_bf.shape[1]
    tok = lambda bb, i: (bb, i, 0)
    const = lambda bb, i: (0, 0)
    widths = (KV_W, KV_W) if keys_only else (Q_W, KV_W, KV_W, Q_W)
    return pl.pallas_call(
        functools.partial(_attn_in_kernel, keys_only=keys_only),
        grid=(b, t // tm),
        in_specs=[pl.BlockSpec((1, tm, D_MODEL), tok),
                  pl.BlockSpec((1, 1, 3 * D_MODEL), _mod_index(is_ctx, b)),
                  pl.BlockSpec((1, D_MODEL), const),
                  pl.BlockSpec((D_MODEL, n_in), const),
                  pl.BlockSpec((MXU_DIM, MXU_DIM), const),
                  pl.BlockSpec((1, Q_W), const),
                  pl.BlockSpec((1, KV_W), const),
                  pl.BlockSpec((tm, LANES), lambda bb, i: (i, 0)),
                  pl.BlockSpec((tm, LANES), lambda bb, i: (i, 0))],
        out_specs=[pl.BlockSpec((1, tm, w), tok) for w in widths],
        out_shape=[jax.ShapeDtypeStruct((b, t, w), BF16) for w in widths],
        compiler_params=_params("arbitrary", "arbitrary"),
        name="attn_in_ctx" if is_ctx else "attn_in",
    )(x, mod, nw, w_bf, e_mat, qn, kn, cos, sin)


V_ROWS = HEAD_DIM + 16

def _attn_kernel(*refs, lens, n_qtiles, n_units, tq, per_step):
    n_src = len(lens)
    q_ref, g_ref = refs[0], refs[1]
    k_refs = refs[2:2 + n_src]
    v_refs = refs[2 + n_src:2 + 2 * n_src]
    o_ref = refs[2 + 2 * n_src]
    k_lo, k_hi, v_t, s_buf, m_buf, o_buf = refs[3 + 2 * n_src:]
    m = pl.program_id(0)
    n_keys = sum(lens)
    half = LANES // 2
    units_per_batch = N_KV_HEADS * n_qtiles
    first = jnp.minimum(per_step * m, n_units - per_step)
    new_batch = (first % units_per_batch == 0) & (per_step * m < n_units)

    @pl.when(m == 0)
    def _fill():
        s_buf[1] = jnp.zeros(s_buf.shape[1:], F32)
        m_buf[1] = jnp.zeros(m_buf.shape[1:], F32)
        o_buf[...] = jnp.ones(o_buf.shape, F32)

    @pl.when(new_batch)
    def _stage():
        zeros = jnp.zeros((n_keys, half), BF16)
        row = lax.broadcasted_iota(jnp.int32, (V_ROWS - HEAD_DIM, n_keys), 0)
        ones_row = jnp.where(row == 0, 1.0, 0.0).astype(BF16)
        v_slot = (first // units_per_batch) % 2
        vts = [v_refs[s][0].astype(F32).T.astype(BF16) for s in range(n_src)]
        for gg in range(N_KV_HEADS):
            k_lo[gg, :, half:] = zeros
            k_hi[gg, :, :half] = zeros
            v_t[v_slot, gg, HEAD_DIM:, :] = ones_row
            off = 0
            for s in range(n_src):
                kk = k_refs[s][0, :, gg * HEAD_DIM:(gg + 1) * HEAD_DIM]
                k_lo[gg, off:off + lens[s], :half] = kk
                k_hi[gg, off:off + lens[s], half:] = kk
                v_t[v_slot, gg, :HEAD_DIM, off:off + lens[s]] = vts[s][gg * HEAD_DIM:(gg + 1) * HEAD_DIM, :]
                off += lens[s]

    def col_max(t):
        while t.shape[0] % (2 * SUBLANES) == 0:
            n = t.shape[0] // 2
            t = jnp.maximum(t[:n, :], t[n:, :])
        return jnp.max(t, axis=0, keepdims=True)

    def sub_step(j):
        write_slot, read_slot = j % 2, (j - 1) % 2
        acc_w, acc_r = (j - 1) % per_step, j
        rows = slice(j * tq, (j + 1) * tq)
        grp = (first // n_qtiles) % N_KV_HEADS
        weighed = jnp.clip(per_step * m + j - 1, 0, n_units - 1)
        vt_g = v_t[(weighed // units_per_batch) % 2, (weighed // n_qtiles) % N_KV_HEADS]

        def score(h):
            q2 = q_ref[0, rows, (h // 2) * LANES:(h // 2 + 1) * LANES]
            k_st = k_hi if h % 2 else k_lo
            s_t = lax.dot_general(k_st[grp], q2, (((1,), (1,)), ((), ())),
                                  preferred_element_type=F32)
            s_buf[write_slot, h] = s_t
            m_buf[write_slot, h] = col_max(s_t)

        def weigh(h):
            p_t = jnp.exp(s_buf[read_slot, h] - m_buf[read_slot, h]).astype(BF16)
            o_buf[acc_w, h] = _dot(vt_g, p_t)

        o_rows = [o_buf[acc_r, h, :HEAD_DIM, :] / o_buf[acc_r, h, HEAD_DIM:HEAD_DIM + 1, :]
                  for h in range(GQA_GROUP)]
        o = jnp.concatenate(o_rows, axis=0).T
        o_ref[0, rows, :] = (o * g_ref[0, rows, :].astype(F32)).astype(BF16)

        for pair in range(GQA_GROUP // 2):
            score(2 * pair)
            weigh(2 * pair)
            weigh(2 * pair + 1)
            score(2 * pair + 1)

    for j in range(per_step):
        sub_step(j)


def _attention(q, g, kvs, name):
    b, t, _ = q.shape
    per_step = max(2, min(ATTN_UNITS_PER_STEP, t // Q_TILE))
    tq = t // max(per_step, t // Q_TILE)
    n_qtiles = t // tq
    n_units = b * N_KV_HEADS * n_qtiles
    n_steps = n_units // per_step
    lens = tuple(int(k.shape[1]) for k, _ in kvs)
    n_keys = sum(lens)

    def step_block(s):
        u = per_step * s
        return (u // (N_KV_HEADS * n_qtiles), (u % n_qtiles) // per_step, (u // n_qtiles) % N_KV_HEADS)

    def step_batch(s):
        return ((per_step * s) // (N_KV_HEADS * n_qtiles), 0, 0)

    scored = lambda m: jnp.minimum(m, n_steps - 1)
    stored = lambda m: jnp.maximum(m - 1, 0)
    width = GQA_GROUP * HEAD_DIM
    in_specs = [pl.BlockSpec((1, per_step * tq, width), lambda m: step_block(scored(m))),
                pl.BlockSpec((1, per_step * tq, width), lambda m: step_block(stored(m)))]
    in_specs += [pl.BlockSpec((1, ln, KV_W), lambda m: step_batch(scored(m))) for ln in lens] * 2
    args = [q, g] + [k for k, _ in kvs] + [v for _, v in kvs]
    return pl.pallas_call(
        functools.partial(_attn_kernel, lens=lens, n_qtiles=n_qtiles, n_units=n_units, tq=tq,
                          per_step=per_step),
        grid=(n_steps + 1,),
        in_specs=in_specs,
        out_specs=pl.BlockSpec((1, per_step * tq, width), lambda m: step_block(stored(m))),
        out_shape=jax.ShapeDtypeStruct((b, t, Q_W), BF16),
        scratch_shapes=[pltpu.VMEM((N_KV_HEADS, n_keys, LANES), BF16),
                        pltpu.VMEM((N_KV_HEADS, n_keys, LANES), BF16),
                        pltpu.VMEM((2, N_KV_HEADS, V_ROWS, n_keys), BF16),
                        pltpu.VMEM((2, GQA_GROUP, n_keys, tq), F32),
                        pltpu.VMEM((2, GQA_GROUP, 1, tq), F32),
                        pltpu.VMEM((per_step, GQA_GROUP, V_ROWS, tq), F32)],
        compiler_params=_params("arbitrary", vmem=VMEM_LIMIT_BIG),
        name=name,
    )(*args)


def _proj_out_kernel(a_ref, x_ref, mod_ref, w_ref, o_ref):
    gate = mod_ref[0][:, 2 * D_MODEL:]
    o_ref[0] = x_ref[0] + gate * _dot(a_ref[0], w_ref[...])


def _proj_out(a, x, mod, w_bf, is_ctx, name):
    b, t, _ = x.shape
    tm = _tile(t)
    tok = lambda bb, i: (bb, i, 0)
    return pl.pallas_call(
        _proj_out_kernel,
        grid=(b, t // tm),
        in_specs=[pl.BlockSpec((1, tm, D_MODEL), tok),
                  pl.BlockSpec((1, tm, D_MODEL), tok),
                  pl.BlockSpec((1, 1, 3 * D_MODEL), _mod_index(is_ctx, b)),
                  pl.BlockSpec((D_MODEL, D_MODEL), lambda bb, i: (0, 0))],
        out_specs=pl.BlockSpec((1, tm, D_MODEL), tok),
        out_shape=jax.ShapeDtypeStruct(x.shape, F32),
        compiler_params=_params("arbitrary", "arbitrary"),
        name=name,
    )(a, x, mod, w_bf)


def _dft_mats(n, scale):
    idx = np.arange(n, dtype=np.int64)
    ang = 2.0 * np.pi * ((idx[:, None] * idx[None, :]) % n).astype(np.float64) / n
    return np.cos(ang) * scale, np.sin(ang) * scale


def _fourier_in_kernel(*refs, after_attention):
    if after_attention:
        a_ref, pmod_ref, pw_ref = refs[:3]
        x_ref, mod_ref, nw_ref, w_ref, cc_ref, sc_ref, xo_ref, uc_ref, g_ref = refs[3:]
        x = x_ref[0] + pmod_ref[0][:, 2 * D_MODEL:] * _dot(a_ref[0], pw_ref[...])
        xo_ref[0] = x
    else:
        x_ref, mod_ref, nw_ref, w_ref, cc_ref, sc_ref, uc_ref, g_ref = refs
        x = x_ref[0]
    h = _modnorm(x, mod_ref[0], nw_ref[...])
    u = _dot(h, w_ref[:, :D_MODEL]).astype(BF16)
    g_ref[0] = _silu(_dot(h, w_ref[:, D_MODEL:])).astype(BF16)
    ucs, uss = [], []
    for c in range(D_MODEL // FOURIER_GROUP_DIM):
        ug = u[:, c * FOURIER_GROUP_DIM:(c + 1) * FOURIER_GROUP_DIM]
        ucs.append(_dot(ug, cc_ref[...]))
        uss.append(_dot(ug, sc_ref[...]))
    uc_ref[0, 0] = jnp.concatenate(ucs, axis=1).astype(BF16)
    uc_ref[0, 1] = jnp.concatenate(uss, axis=1).astype(BF16)


def _fourier_in(x, mod, nw, w_bf, cmat, smat, is_ctx, prev=None):
    b, t, _ = x.shape
    tm = _tile(t)
    tok = lambda bb, i: (bb, i, 0)
    const = lambda bb, i: (0, 0)
    mod_spec = pl.BlockSpec((1, 1, 3 * D_MODEL), _mod_index(is_ctx, b))
    in_specs = [pl.BlockSpec((1, tm, D_MODEL), tok),
                mod_spec,
                pl.BlockSpec((1, D_MODEL), const),
                pl.BlockSpec((D_MODEL, 2 * D_MODEL), const),
                pl.BlockSpec((FOURIER_GROUP_DIM, FOURIER_GROUP_DIM), const),
                pl.BlockSpec((FOURIER_GROUP_DIM, FOURIER_GROUP_DIM), const)]
    out_specs = [pl.BlockSpec((1, 2, tm, D_MODEL), lambda bb, i: (bb, 0, i, 0)),
                 pl.BlockSpec((1, tm, D_MODEL), tok)]
    out_shape = [jax.ShapeDtypeStruct((b, 2, t, D_MODEL), BF16),
                 jax.ShapeDtypeStruct((b, t, D_MODEL), BF16)]
    args = (x, mod, nw, w_bf, cmat, smat)
    if prev is not None:
        in_specs = [pl.BlockSpec((1, tm, D_MODEL), tok), mod_spec,
                    pl.BlockSpec((D_MODEL, D_MODEL), const)] + in_specs
        out_specs = [pl.BlockSpec((1, tm, D_MODEL), tok)] + out_specs
        out_shape = [jax.ShapeDtypeStruct(x.shape, F32)] + out_shape
        args = tuple(prev) + args
    return pl.pallas_call(
        functools.partial(_fourier_in_kernel, after_attention=prev is not None),
        grid=(b, t // tm),
        in_specs=in_specs,
        out_specs=out_specs,
        out_shape=out_shape,
        compiler_params=_params("arbitrary", "arbitrary"),
        name="fourier_in_ctx" if is_ctx else "fourier_in",
    )(*args)


def _fourier_out_kernel(dft_ref, rev_ref, uc_ref, g_ref, x_ref, mod_ref, w_ref, o_ref, folded):
    n_pos = uc_ref.shape[1] // 2
    half = n_pos // 2
    tm = o_ref.shape[1]
    rt = rev_ref.shape[0]

    @pl.when(pl.program_id(1) == 0)
    def _fold():
        first_row = lax.broadcasted_iota(jnp.int32, (half, D_MODEL), 0) == 0
        for part, sign in ((0, 1.0), (1, -1.0)):
            base = part * n_pos
            tiles = [_dot(rev_ref[...], uc_ref[0, pl.ds(base + n_pos - (j + 1) * rt, rt), :])
                     for j in range(half // rt)]
            rev = jnp.concatenate(tiles, axis=0)
            mirror = jnp.where(first_row, 0.0, pltpu.roll(rev, 1, 0))
            folded[pl.ds(part * half, half), :] = (
                uc_ref[0, pl.ds(base, half), :].astype(F32) + sign * mirror).astype(BF16)

    mid = uc_ref[0, half:half + 2 * SUBLANES, :].astype(F32)[0:1, :] * (n_pos ** -0.5)
    gate = mod_ref[0][:, 2 * D_MODEL:]
    chunk = min(tm, MXU_DIM)
    spans = [slice(r, r + chunk) for r in range(0, tm, chunk)]
    odd = (lax.broadcasted_iota(jnp.int32, (chunk, D_MODEL), 0) & 1) == 1
    row0 = pl.multiple_of(pl.program_id(1) * tm, chunk)
    fs = [_dot(dft_ref[pl.ds(row0 + rows.start, chunk), :], folded[...]) for rows in spans]
    for rows, f in zip(spans, fs):
        f = f + jnp.where(odd, -mid, mid)
        a = (f * g_ref[0, rows, :].astype(F32)).astype(BF16)
        o_ref[0, rows, :] = x_ref[0, rows, :] + gate * _dot(a, w_ref[...])


def _fourier_out(dft, rev, ucs, g, x, mod, w_bf, is_ctx):
    b, t, _ = x.shape
    tm = _tile(t, 2 * SEQ_TILE)
    tok = lambda bb, i: (bb, i, 0)
    return pl.pallas_call(
        _fourier_out_kernel,
        grid=(b, t // tm),
        in_specs=[pl.BlockSpec((t, t), lambda bb, i: (0, 0), pipeline_mode=pl.Buffered(1)),
                  pl.BlockSpec(rev.shape, lambda bb, i: (0, 0)),
                  pl.BlockSpec((1, 2 * t, D_MODEL), lambda bb, i: (bb, 0, 0)),
                  pl.BlockSpec((1, tm, D_MODEL), tok),
                  pl.BlockSpec((1, tm, D_MODEL), tok),
                  pl.BlockSpec((1, 1, 3 * D_MODEL), _mod_index(is_ctx, b)),
                  pl.BlockSpec((D_MODEL, D_MODEL), lambda bb, i: (0, 0))],
        out_specs=pl.BlockSpec((1, tm, D_MODEL), tok),
        out_shape=jax.ShapeDtypeStruct(x.shape, F32),
        scratch_shapes=[pltpu.VMEM((t, D_MODEL), BF16)],
        compiler_params=_params("arbitrary", "arbitrary", vmem=VMEM_LIMIT_BIG),
        name="fourier_out_ctx" if is_ctx else "fourier_out",
    )(dft, rev, ucs.reshape(b, 2 * t, D_MODEL), g, x, mod, w_bf)


def _conv_in_kernel(x_ref, mod_ref, nw_ref, w_ref, u_ref, g_ref):
    h = _modnorm(x_ref[0], mod_ref[0], nw_ref[...])
    a = _dot(h, w_ref[:, :D_MODEL])
    gl = _dot(h, w_ref[:, D_MODEL:2 * D_MODEL])
    u_ref[0] = a * _sigmoid(gl)
    g_ref[0] = _silu(_dot(h, w_ref[:, 2 * D_MODEL:])).astype(BF16)


def _conv_in(x, mod, nw, w_bf, is_ctx):
    b, t, _ = x.shape
    tm = _tile(t)
    tok = lambda bb, i: (bb, i, 0)
    const = lambda bb, i: (0, 0)
    return pl.pallas_call(
        _conv_in_kernel,
        grid=(b, t // tm),
        in_specs=[pl.BlockSpec((1, tm, D_MODEL), tok),
                  pl.BlockSpec((1, 1, 3 * D_MODEL), _mod_index(is_ctx, b)),
                  pl.BlockSpec((1, D_MODEL), const),
                  pl.BlockSpec((D_MODEL, 3 * D_MODEL), const)],
        out_specs=[pl.BlockSpec((1, tm, D_MODEL), tok),
                   pl.BlockSpec((1, tm, D_MODEL), tok)],
        out_shape=[jax.ShapeDtypeStruct((b, t, D_MODEL), F32),
                   jax.ShapeDtypeStruct((b, t, D_MODEL), BF16)],
        compiler_params=_params("arbitrary", "arbitrary"),
        name="conv_in_ctx" if is_ctx else "conv_in",
    )(x, mod, nw, w_bf)


def _conv_out_kernel(u_ref, g_ref, x_ref, mod_ref, dw_ref, dwb_ref, lnw_ref, lnb_ref, w_ref,
                     o_ref, upad, conv):
    t = u_ref.shape[1]
    tm = o_ref.shape[1]
    halo = CONV_HALO
    n_cols = D_MODEL // LANES
    i = pl.program_id(1)

    @pl.when(i == 0)
    def _stage():
        for col in range(n_cols):
            upad[col, 0:halo, :] = jnp.zeros((halo, LANES), F32)
            upad[col, halo:halo + t, :] = u_ref[0, :, col * LANES:(col + 1) * LANES]
            upad[col, halo + t:, :] = jnp.zeros((halo, LANES), F32)

    groups = CONV_BLOCK_ROWS // SUBLANES
    row0 = pl.multiple_of(i * tm, SUBLANES)

    def block(n, carry):
        col = n % n_cols
        base = pl.multiple_of((n // n_cols) * CONV_BLOCK_ROWS, CONV_BLOCK_ROWS)
        acc = jnp.zeros((groups, SUBLANES, LANES), F32)
        for k in range(CONV_WIDTH):
            rows = upad[col, pl.ds(row0 + base + (halo - CONV_PAD + k), CONV_BLOCK_ROWS), :]
            acc = acc + rows.reshape(groups, SUBLANES, LANES) * dw_ref[k, col][None]
        conv[col, pl.ds(base, CONV_BLOCK_ROWS), :] = acc.reshape(CONV_BLOCK_ROWS, LANES)
        return carry

    lax.fori_loop(0, (tm // CONV_BLOCK_ROWS) * n_cols, block, 0, unroll=4)

    c = jnp.concatenate([conv[col] for col in range(n_cols)], axis=1) + dwb_ref[...]
    mu = jnp.mean(c, axis=-1, keepdims=True)
    cen = c - mu
    var = jnp.mean(cen * cen, axis=-1, keepdims=True)
    ln = cen * lax.rsqrt(var + EPS) * lnw_ref[...] + lnb_ref[...]
    a = (_silu(ln) * g_ref[0].astype(F32)).astype(BF16)
    gate = mod_ref[0][:, 2 * D_MODEL:]
    o_ref[0] = x_ref[0] + gate * _dot(a, w_ref[...])


def _conv_out(u, g, x, mod, dw_w, dw_b, ln_w, ln_b, w_bf, is_ctx):
    b, t, _ = x.shape
    tm = _tile(t, SEQ_TILE)
    dw_tiles = jnp.broadcast_to(dw_w.reshape(CONV_WIDTH, D_MODEL // LANES, 1, LANES),
                                (CONV_WIDTH, D_MODEL // LANES, SUBLANES, LANES))
    tok = lambda bb, i: (bb, i, 0)
    const = lambda bb, i: (0, 0)
    return pl.pallas_call(
        _conv_out_kernel,
        grid=(b, t // tm),
        in_specs=[pl.BlockSpec((1, t, D_MODEL), lambda bb, i: (bb, 0, 0)),
                  pl.BlockSpec((1, tm, D_MODEL), tok),
                  pl.BlockSpec((1, tm, D_MODEL), tok),
                  pl.BlockSpec((1, 1, 3 * D_MODEL), _mod_index(is_ctx, b)),
                  pl.BlockSpec((CONV_WIDTH, D_MODEL // LANES, SUBLANES, LANES),
                               lambda bb, i: (0, 0, 0, 0)),
                  pl.BlockSpec((1, D_MODEL), const),
                  pl.BlockSpec((1, D_MODEL), const),
                  pl.BlockSpec((1, D_MODEL), const),
                  pl.BlockSpec((D_MODEL, D_MODEL), const)],
        out_specs=pl.BlockSpec((1, tm, D_MODEL), tok),
        out_shape=jax.ShapeDtypeStruct(x.shape, F32),
        scratch_shapes=[pltpu.VMEM((D_MODEL // LANES, t + 2 * CONV_HALO, LANES), F32),
                        pltpu.VMEM((D_MODEL // LANES, tm, LANES), F32)],
        compiler_params=_params("arbitrary", "arbitrary", vmem=VMEM_LIMIT_BIG),
        name="conv_out_ctx" if is_ctx else "conv_out",
    )(u, g, x, mod, dw_tiles, dw_b.reshape(1, -1), ln_w.reshape(1, -1), ln_b.reshape(1, -1), w_bf)


def _rope_tables(n_tokens):
    pos = jnp.arange(n_tokens)
    pos2 = jnp.stack([pos // GRID_W, pos % GRID_W], axis=-1).astype(F32)
    inv_freq = ROPE_THETA ** (-jnp.arange(ROPE_FREQS, dtype=F32) / ROPE_FREQS)
    ang = pos2[:, :, None] * inv_freq
    cos, sin = jnp.cos(ang), jnp.sin(ang)
    cos64 = jnp.concatenate([cos[:, 0], cos[:, 0], cos[:, 1], cos[:, 1]], axis=-1)
    sin64 = jnp.concatenate([-sin[:, 0], sin[:, 0], -sin[:, 1], sin[:, 1]], axis=-1)
    return jnp.tile(cos64, (1, 2)), jnp.tile(sin64, (1, 2))


def _head_mean_matrix():
    idx = np.arange(MXU_DIM) // HEAD_DIM
    return jnp.asarray((idx[:, None] == idx[None, :]).astype(np.float32) / HEAD_DIM, dtype=BF16)


def kernel(x, c, ctx, c_ctx, l0_ada_w, l0_ada_b, l0_norm_w, l0_w_in, l0_q_norm, l0_k_norm, l0_w_out, l1_ada_w, l1_ada_b, l1_norm_w, l1_w_in, l1_w_out, l2_ada_w, l2_ada_b, l2_norm_w, l2_w_in, l2_dw_w, l2_dw_b, l2_ln_w, l2_ln_b, l2_w_out, l3_ada_w, l3_ada_b, l3_norm_w, l3_w_in, l3_q_norm, l3_k_norm, l3_w_out):
    n_batch, seq, _ = x.shape
    n_ctx = ctx.shape[1]

    rows = -(-(n_batch + 1) // SUBLANES) * SUBLANES
    cc = jnp.concatenate([c, c_ctx[None, :], jnp.zeros((rows - n_batch - 1, D_MODEL), F32)], axis=0)

    def mods(ada_w, ada_b):
        return _ada(cc, ada_w, ada_b).reshape(rows, 1, 3 * D_MODEL)

    cos, sin = _rope_tables(seq)
    cos_ctx = jnp.ones((n_ctx, LANES), F32)
    sin_ctx = jnp.zeros((n_ctx, LANES), F32)
    e_mat = _head_mean_matrix()

    def attention_layer(xl, xc, ada_w, ada_b, nw, w_in, qn, kn, w_out, need_ctx, defer_out=False):
        mod = mods(ada_w, ada_b)
        nw2 = nw.reshape(1, -1)
        w_bf = w_in.astype(BF16)
        wo_bf = w_out.astype(BF16)
        qn_t = jnp.tile(qn, N_HEADS).reshape(1, -1)
        kn_t = jnp.tile(kn, N_KV_HEADS).reshape(1, -1)
        q, k, v, g = _attn_in(xl, mod, nw2, w_bf, e_mat, qn_t, kn_t, cos, sin, False)
        ctx_out = _attn_in(xc, mod, nw2, w_bf, e_mat, qn_t, kn_t, cos_ctx, sin_ctx, True,
                           keys_only=not need_ctx)
        kc, vc = (ctx_out[1], ctx_out[2]) if need_ctx else ctx_out
        a = _attention(q, g, [(kc, vc), (k, v)], "attention")
        if defer_out:
            ac = _attention(ctx_out[0], ctx_out[3], [(kc, vc)], "attention_ctx")
            return (a, mod, wo_bf), (ac, mod, wo_bf)
        xl_new = _proj_out(a, xl, mod, wo_bf, False, "attn_out")
        if need_ctx:
            ac = _attention(ctx_out[0], ctx_out[3], [(kc, vc)], "attention_ctx")
            xc = _proj_out(ac, xc, mod, wo_bf, True, "attn_out_ctx")
        return xl_new, xc

    def fourier_layer(xl, xc, ada_w, ada_b, nw, w_in, w_out, prev_l=None, prev_c=None):
        mod = mods(ada_w, ada_b)
        nw2 = nw.reshape(1, -1)
        w_bf = w_in.astype(BF16)
        wo_bf = w_out.astype(BF16)
        cch, sch = _dft_mats(FOURIER_GROUP_DIM, FOURIER_GROUP_DIM ** -0.5)
        cch, sch = jnp.asarray(cch, dtype=BF16), jnp.asarray(sch, dtype=BF16)
        outs = []
        for t_arr, is_ctx, prev in ((xl, False, prev_l), (xc, True, prev_c)):
            t = t_arr.shape[1]
            cp, sp = _dft_mats(t, t ** -0.5)
            dft = jnp.asarray(np.concatenate([cp[:, :t // 2], -sp[:, :t // 2]], axis=1), dtype=BF16)
            rt = min(MXU_DIM, t // 2)
            rev = jnp.asarray(np.eye(rt)[::-1], dtype=BF16)
            res = _fourier_in(t_arr, mod, nw2, w_bf, cch, sch, is_ctx, prev)
            if prev is not None:
                t_arr, res = res[0], res[1:]
            ucs, g = res
            outs.append(_fourier_out(dft, rev, ucs, g, t_arr, mod, wo_bf, is_ctx))
        return outs[0], outs[1]

    def conv_layer(xl, xc, ada_w, ada_b, nw, w_in, dw_w, dw_b, ln_w, ln_b, w_out):
        mod = mods(ada_w, ada_b)
        nw2 = nw.reshape(1, -1)
        w_bf = w_in.astype(BF16)
        wo_bf = w_out.astype(BF16)
        outs = []
        for t_arr, is_ctx in ((xl, False), (xc, True)):
            u, g = _conv_in(t_arr, mod, nw2, w_bf, is_ctx)
            outs.append(_conv_out(u, g, t_arr, mod, dw_w, dw_b, ln_w, ln_b, wo_bf, is_ctx))
        return outs[0], outs[1]

    xl, xc = x, ctx
    prev_l, prev_c = attention_layer(xl, xc, l0_ada_w, l0_ada_b, l0_norm_w, l0_w_in, l0_q_norm,
                                     l0_k_norm, l0_w_out, True, defer_out=True)
    xl, xc = fourier_layer(xl, xc, l1_ada_w, l1_ada_b, l1_norm_w, l1_w_in, l1_w_out, prev_l, prev_c)
    xl, xc = conv_layer(xl, xc, l2_ada_w, l2_ada_b, l2_norm_w, l2_w_in, l2_dw_w, l2_dw_b,
                        l2_ln_w, l2_ln_b, l2_w_out)
    xl, xc = attention_layer(xl, xc, l3_ada_w, l3_ada_b, l3_norm_w, l3_w_in, l3_q_norm, l3_k_norm,
                             l3_w_out, False)
    return xl
```
